```python
import jax, jax.numpy as jnp
from jax import lax
import numpy as np

D_MODEL = 1024
BATCH = 2
SEQ = 16384
DEPTH = 2

CHUNK = 64
Q_BLOCK = 128
EPS = 1e-6
GDN_WIDTH = D_MODEL // 2
GDN_HEAD_DIM = 128
GDN_HEADS = GDN_WIDTH // GDN_HEAD_DIM
CONV_K = 4
FOX_WIDTH = D_MODEL - GDN_WIDTH
FOX_HEAD_DIM = 64
FOX_HEADS = FOX_WIDTH // FOX_HEAD_DIM
MEM_TOKENS = 256
MEM_HEADS = 4
MEM_HEAD_DIM = D_MODEL // MEM_HEADS
PEER_HEADS = 8
PEER_KEYS = 128
PEER_EXPERTS = PEER_KEYS * PEER_KEYS
PEER_QDIM = 256
PEER_HALF = PEER_QDIM // 2
PEER_TOPK = 16
PEER_TOKEN_BLOCK = 128
IN_COLS = 3 * GDN_WIDTH + GDN_WIDTH + 2 * GDN_HEADS + 3 * FOX_WIDTH + FOX_HEADS

kernel_name = "hybrid_gdn_fox_peer_trunk"


def rms_norm(x, gain):
    xf = x.astype(jnp.float32)
    y = xf * lax.rsqrt(jnp.mean(xf * xf, axis=-1, keepdims=True) + EPS)
    return (y * gain.astype(jnp.float32)).astype(x.dtype)


def l2_norm(x):
    xf = x.astype(jnp.float32)
    return xf * lax.rsqrt(jnp.sum(xf * xf, axis=-1, keepdims=True) + EPS)


def causal_depthwise_conv(x, w):
    return lax.conv_general_dilated(x, w.astype(x.dtype), window_strides=(1,),
                                    padding=[(CONV_K - 1, 0)],
                                    dimension_numbers=('NWC', 'WIO', 'NWC'),
                                    feature_group_count=x.shape[-1])


def gated_delta_rule(q, k, v, log_alpha, beta):
    B, T, H, dk = q.shape
    dv = v.shape[-1]
    n = T // CHUNK

    def chunks(a):
        return jnp.swapaxes(a.reshape(B, n, CHUNK, H, *a.shape[3:]), 2, 3)

    q = chunks(q) * (dk ** -0.5)
    k = chunks(k)
    v = chunks(v)
    g = jnp.cumsum(chunks(log_alpha), axis=-1)
    b = chunks(beta)
    causal = jnp.tril(jnp.ones((CHUNK, CHUNK), bool))
    strict = jnp.tril(jnp.ones((CHUNK, CHUNK), bool), -1)
    diff = g[..., :, None] - g[..., None, :]
    decay = jnp.where(causal, jnp.exp(jnp.where(causal, diff, 0.0)), 0.0)
    k_beta = k * b[..., None]
    a = jnp.where(strict, jnp.einsum('bnhik,bnhjk->bnhij', k_beta, k) * decay, 0.0)
    eye = jnp.eye(CHUNK, dtype=jnp.float32)
    t_inv = lax.linalg.triangular_solve(eye + a, jnp.broadcast_to(eye, a.shape),
                                        left_side=True, lower=True)
    w = jnp.einsum('bnhij,bnhjk->bnhik', t_inv, k_beta * jnp.exp(g)[..., None])
    u = jnp.einsum('bnhij,bnhjv->bnhiv', t_inv, v * b[..., None])
    qk = jnp.where(causal, jnp.einsum('bnhik,bnhjk->bnhij', q, k) * decay, 0.0)
    q_dec = q * jnp.exp(g)[..., None]
    k_dec = k * jnp.exp(g[..., -1:] - g)[..., None]
    g_last = jnp.exp(g[..., -1])

    def step(state, inp):
        w_c, u_c, qd_c, kd_c, qk_c, gl_c = inp
        v_new = u_c - jnp.einsum('bhck,bhkv->bhcv', w_c, state)
        o = jnp.einsum('bhck,bhkv->bhcv', qd_c, state) + jnp.einsum('bhij,bhjv->bhiv', qk_c, v_new)
        state = state * gl_c[..., None, None] + jnp.einsum('bhck,bhcv->bhkv', kd_c, v_new)
        return state, o

    xs = tuple(jnp.moveaxis(t, 1, 0) for t in (w, u, q_dec, k_dec, qk, g_last))
    s0 = jnp.zeros((B, H, dk, dv), jnp.float32)
    _, o = lax.scan(step, s0, xs)
    return jnp.swapaxes(jnp.moveaxis(o, 0, 1), 2, 3).reshape(B, T, H, dv)


def forgetting_attention(q, k, v, log_f):
    B, T, H, d = q.shape
    nb = T // Q_BLOCK
    c = jnp.cumsum(log_f, axis=1).transpose(0, 2, 1)
    kh = k.transpose(0, 2, 1, 3)
    vh = v.transpose(0, 2, 1, 3)
    qb = q.transpose(0, 2, 1, 3).reshape(B, H, nb, Q_BLOCK, d).transpose(2, 0, 1, 3, 4)
    cb = c.reshape(B, H, nb, Q_BLOCK).transpose(2, 0, 1, 3)
    pos_k = jnp.arange(T)
    scale = d ** -0.5

    def block(args):
        q_i, c_i, i = args
        pos_q = i * Q_BLOCK + jnp.arange(Q_BLOCK)
        s = jnp.einsum('bhqd,bhkd->bhqk', q_i, kh) * scale + c_i[..., :, None] - c[..., None, :]
        s = jnp.where(pos_k[None, :] <= pos_q[:, None], s, -jnp.inf)
        p = jax.nn.softmax(s, axis=-1)
        return jnp.einsum('bhqk,bhkd->bhqd', p, vh)

    o = lax.map(block, (qb, cb, jnp.arange(nb)))
    return o.transpose(1, 0, 3, 2, 4).reshape(B, T, H, d)


def hybrid_mixer(h, w_in, conv_w, a_log, dt_bias, gdn_out_norm, fox_q_norm, fox_k_norm, fox_f_bias, w_out):
    B, T, _ = h.shape
    f32 = jnp.float32
    proj = h @ w_in
    o1 = 3 * GDN_WIDTH
    o2 = o1 + GDN_WIDTH
    o3 = o2 + GDN_HEADS
    o4 = o3 + GDN_HEADS
    o5 = o4 + 3 * FOX_WIDTH
    gdn_qkv, gdn_gate, gdn_a, gdn_b, fox_qkv, fox_f = jnp.split(proj, [o1, o2, o3, o4, o5], axis=-1)
    gdn_qkv = jax.nn.silu(causal_depthwise_conv(gdn_qkv, conv_w))
    gq, gk, gv = jnp.split(gdn_qkv, 3, axis=-1)
    gq = l2_norm(gq.reshape(B, T, GDN_HEADS, GDN_HEAD_DIM))
    gk = l2_norm(gk.reshape(B, T, GDN_HEADS, GDN_HEAD_DIM))
    gv = gv.reshape(B, T, GDN_HEADS, GDN_HEAD_DIM).astype(f32)
    log_alpha = -jnp.exp(a_log.astype(f32)) * jax.nn.softplus(gdn_a.astype(f32) + dt_bias.astype(f32))
    beta = jax.nn.sigmoid(gdn_b.astype(f32))
    o_gdn = gated_delta_rule(gq, gk, gv, log_alpha, beta)
    o_gdn = rms_norm(o_gdn, gdn_out_norm) * jax.nn.silu(
        gdn_gate.reshape(B, T, GDN_HEADS, GDN_HEAD_DIM).astype(f32))
    fq, fk, fv = jnp.split(fox_qkv, 3, axis=-1)
    fq = rms_norm(fq.reshape(B, T, FOX_HEADS, FOX_HEAD_DIM), fox_q_norm).astype(f32)
    fk = rms_norm(fk.reshape(B, T, FOX_HEADS, FOX_HEAD_DIM), fox_k_norm).astype(f32)
    fv = fv.reshape(B, T, FOX_HEADS, FOX_HEAD_DIM).astype(f32)
    log_f = jax.nn.log_sigmoid(fox_f.astype(f32) + fox_f_bias.astype(f32))
    o_fox = forgetting_attention(fq, fk, fv, log_f)
    y = jnp.concatenate([o_gdn.reshape(B, T, GDN_WIDTH), o_fox.reshape(B, T, FOX_WIDTH)], axis=-1)
    return y.astype(h.dtype) @ w_out


def memory_cross_attention(h, mem, mem_norm, w_mq, w_mkv, mq_norm, mk_norm, w_mo):
    B, T, D = h.shape
    M = mem.shape[1]
    f32 = jnp.float32
    mem_h = rms_norm(mem, mem_norm)
    q = rms_norm((h @ w_mq).reshape(B, T, MEM_HEADS, MEM_HEAD_DIM), mq_norm).astype(f32)
    k, v = jnp.split(mem_h @ w_mkv, 2, axis=-1)
    k = rms_norm(k.reshape(B, M, MEM_HEADS, MEM_HEAD_DIM), mk_norm).astype(f32)
    v = v.reshape(B, M, MEM_HEADS, MEM_HEAD_DIM).astype(f32)
    s = jnp.einsum('bthd,bmhd->bhtm', q, k) * (MEM_HEAD_DIM ** -0.5)
    p = jax.nn.softmax(s, axis=-1)
    o = jnp.einsum('bhtm,bmhd->bthd', p, v).reshape(B, T, D).astype(h.dtype)
    return o @ w_mo


def peer_ffn(h, w_pq, sub_keys, expert_u, expert_v):
    B, T, D = h.shape
    M = B * T
    hf = h.reshape(M, D)
    q = (hf @ w_pq).reshape(M, PEER_HEADS, 2, PEER_HALF)
    s1 = jnp.einsum('mhd,hnd->mhn', q[:, :, 0], sub_keys[0]).astype(jnp.float32)
    s2 = jnp.einsum('mhd,hnd->mhn', q[:, :, 1], sub_keys[1]).astype(jnp.float32)
    v1, i1 = lax.top_k(s1, PEER_TOPK)
    v2, i2 = lax.top_k(s2, PEER_TOPK)
    cand_s = (v1[..., :, None] + v2[..., None, :]).reshape(M, PEER_HEADS, PEER_TOPK * PEER_TOPK)
    cand_i = (i1[..., :, None] * PEER_KEYS + i2[..., None, :]).reshape(M, PEER_HEADS, PEER_TOPK * PEER_TOPK)
    top_s, top_pos = lax.top_k(cand_s, PEER_TOPK)
    idx = jnp.take_along_axis(cand_i, top_pos, axis=-1)
    gate = jax.nn.softmax(top_s, axis=-1)
    nb = M // PEER_TOKEN_BLOCK

    def block(args):
        x_b, idx_b, g_b = args
        u = expert_u[idx_b]
        act = jax.nn.gelu(jnp.einsum('mhkd,md->mhk', u, x_b).astype(jnp.float32))
        return jnp.einsum('mhk,mhkd->md', act * g_b, expert_v[idx_b].astype(jnp.float32))

    y = lax.map(block, (hf.reshape(nb, PEER_TOKEN_BLOCK, D),
                        idx.reshape(nb, PEER_TOKEN_BLOCK, PEER_HEADS, PEER_TOPK),
                        gate.reshape(nb, PEER_TOKEN_BLOCK, PEER_HEADS, PEER_TOPK)))
    return y.reshape(B, T, D).astype(h.dtype)


def setup_inputs(seed: int = 0) -> dict:
    key = jax.random.key(seed)
    ks = jax.random.split(key, 26)
    nrm = jax.random.normal
    D = D_MODEL

    def gain(k, n):
        return 1.0 + 0.02 * nrm(k, (DEPTH, n), jnp.float32)

    dt = jnp.exp(jax.random.uniform(ks[5], (DEPTH, GDN_HEADS), jnp.float32,
                                    minval=np.log(1e-3), maxval=np.log(1e-1)))
    return {
        "x": nrm(ks[0], (BATCH, SEQ, D), jnp.float32),
        "mem": nrm(ks[1], (BATCH, MEM_TOKENS, D), jnp.float32),
        "mix_norm": gain(ks[2], D),
        "w_in": nrm(ks[3], (DEPTH, D, IN_COLS), jnp.float32) * D ** -0.5,
        "conv_w": nrm(ks[4], (DEPTH, CONV_K, 1, 3 * GDN_WIDTH), jnp.float32) * CONV_K ** -0.5,
        "a_log": jnp.log(jax.random.uniform(ks[6], (DEPTH, GDN_HEADS), jnp.float32, minval=1.0, maxval=16.0)),
        "dt_bias": dt + jnp.log(-jnp.expm1(-dt)),
        "gdn_out_norm": gain(ks[7], GDN_HEAD_DIM),
        "fox_q_norm": gain(ks[8], FOX_HEAD_DIM),
        "fox_k_norm": gain(ks[9], FOX_HEAD_DIM),
        "fox_f_bias": 2.0 + 0.5 * nrm(ks[10], (DEPTH, FOX_HEADS), jnp.float32),
        "w_out": nrm(ks[11], (DEPTH, D, D), jnp.float32) * D ** -0.5,
        "xattn_norm": gain(ks[12], D),
        "mem_norm": gain(ks[13], D),
        "w_mq": nrm(ks[14], (DEPTH, D, D), jnp.float32) * D ** -0.5,
        "w_mkv": nrm(ks[15], (DEPTH, D, 2 * D), jnp.float32) * D ** -0.5,
        "mq_norm": gain(ks[16], MEM_HEAD_DIM),
        "mk_norm": gain(ks[17], MEM_HEAD_DIM),
        "w_mo": nrm(ks[18], (DEPTH, D, D), jnp.float32) * D ** -0.5,
        "ffn_norm": gain(ks[19], D),
        "w_pq": nrm(ks[20], (DEPTH, D, PEER_HEADS * PEER_QDIM), jnp.float32) * D ** -0.5,
        "sub_keys": nrm(ks[21], (DEPTH, 2, PEER_HEADS, PEER_KEYS, PEER_HALF), jnp.float32) * PEER_HALF ** -0.5,
        "expert_u": nrm(ks[22], (DEPTH, PEER_EXPERTS, D), jnp.float32) * D ** -0.5,
        "expert_v": nrm(ks[23], (DEPTH, PEER_EXPERTS, D), jnp.float32) * PEER_HEADS ** -0.5,
    }


def reference(x, mem, mix_norm, w_in, conv_w, a_log, dt_bias, gdn_out_norm, fox_q_norm, fox_k_norm,
              fox_f_bias, w_out, xattn_norm, mem_norm, w_mq, w_mkv, mq_norm, mk_norm, w_mo,
              ffn_norm, w_pq, sub_keys, expert_u, expert_v):
    for l in range(DEPTH):
        h = rms_norm(x, mix_norm[l])
        x = x + hybrid_mixer(h, w_in[l], conv_w[l], a_log[l], dt_bias[l], gdn_out_norm[l],
                             fox_q_norm[l], fox_k_norm[l], fox_f_bias[l], w_out[l])
        h = rms_norm(x, xattn_norm[l])
        x = x + memory_cross_attention(h, mem, mem_norm[l], w_mq[l], w_mkv[l], mq_norm[l], mk_norm[l], w_mo[l])
        h = rms_norm(x, ffn_norm[l])
        x = x + peer_ffn(h, w_pq[l], sub_keys[l], expert_u[l], expert_v[l])
    return x
```

```python
import functools

import jax
import jax.numpy as jnp
from jax import lax
from jax.experimental import pallas as pl
from jax.experimental.pallas import tpu as pltpu

F32 = jnp.float32
BF16 = jnp.bfloat16
EPS = 1e-6
NEG_INF = float("-inf")

D_MODEL = 1024
CHUNK = 64
GDN_HEADS = 4
GDN_DIM = 128
GDN_WIDTH = GDN_HEADS * GDN_DIM
CONV_K = 4
FOX_HEADS = 8
FOX_DIM = 64
FOX_WIDTH = FOX_HEADS * FOX_DIM
MEM_HEADS = 4
MEM_DIM = 256
PEER_HEADS = 8
PEER_KEYS = 128
PEER_HALF = 128
PEER_TOPK = 16
PEER_EXPERTS = PEER_KEYS * PEER_KEYS
LANES = 128
HALO = 8

TOK_TILE = 512
FOX_BLK = 512
PEER_TOK = 512
PEER_ECHUNK = 2048
PEER_SUB = 1024
VMEM_LIMIT = 56 * 1024 * 1024


def _dot(a, b):
    return jnp.dot(a.astype(BF16), b.astype(BF16), preferred_element_type=F32)


def _dot_nt(a, b):
    return lax.dot_general(a.astype(BF16), b.astype(BF16), (((1,), (1,)), ((), ())),
                           preferred_element_type=F32)


def _split2(x):
    hi = x.astype(BF16)
    lo = (x - hi.astype(F32)).astype(BF16)
    return hi, lo


def _split3(x):
    hi = x.astype(BF16)
    r = x - hi.astype(F32)
    mid = r.astype(BF16)
    lo = (r - mid.astype(F32)).astype(BF16)
    return hi, mid, lo


def _dot_sel_left(sel, x):
    hi, mid, lo = _split3(x)
    d = functools.partial(jnp.dot, preferred_element_type=F32)
    return d(sel, hi) + d(sel, mid) + d(sel, lo)


def _dot_sel_right(x, sel, pieces=2):
    d = functools.partial(jnp.dot, preferred_element_type=F32)
    if pieces == 2:
        hi, lo = _split2(x)
        return d(hi, sel) + d(lo, sel)
    hi, mid, lo = _split3(x)
    return d(hi, sel) + d(mid, sel) + d(lo, sel)


def _sigmoid(x):
    return 1.0 / (1.0 + jnp.exp(-x))


def _silu(x):
    return x * _sigmoid(x)


def _softplus(x):
    return jnp.maximum(x, 0.0) + jnp.log(1.0 + jnp.exp(-jnp.abs(x)))


def _gelu_tanh(x):
    c = 0.7978845608028654
    return 0.5 * x * (1.0 + jnp.tanh(c * (x + 0.044715 * (x * x * x))))


def _rms_rows(x, gain):
    ms = jnp.mean(x * x, axis=-1, keepdims=True)
    return x * lax.rsqrt(ms + EPS) * gain


def _iota2(shape, dim):
    return lax.broadcasted_iota(jnp.int32, shape, dim)


def _params(sem):
    return pltpu.CompilerParams(dimension_semantics=sem, vmem_limit_bytes=VMEM_LIMIT)


def _gdn_in_kernel(x_ref, g_ref, w_ref, aneg_ref, dtb_ref, qkv_ref, gate_ref, la_ref, beta_ref):
    h = _rms_rows(x_ref[0], g_ref[...]).astype(BF16)
    o_gate = 3 * GDN_WIDTH
    o_a = o_gate + GDN_WIDTH
    o_b = o_a + GDN_WIDTH
    qkv_ref[0] = jnp.dot(h, w_ref[:, :o_gate], preferred_element_type=F32)
    gate_ref[0] = _silu(jnp.dot(h, w_ref[:, o_gate:o_a], preferred_element_type=F32))
    a = jnp.dot(h, w_ref[:, o_a:o_b], preferred_element_type=F32)
    la_ref[0] = aneg_ref[...] * _softplus(a + dtb_ref[...])
    b = jnp.dot(h, w_ref[:, o_b:], preferred_element_type=F32)
    beta_ref[0] = _sigmoid(b)


def _gdn_in(x, gain, w, aneg, dtb):
    B, T, D = x.shape
    n = w.shape[1]
    tm = TOK_TILE
    row = lambda b, i: (b, i, 0)
    fixed = lambda b, i: (0, 0)
    return pl.pallas_call(
        _gdn_in_kernel,
        grid=(B, T // tm),
        in_specs=[pl.BlockSpec((1, tm, D), row), pl.BlockSpec((1, D), fixed),
                  pl.BlockSpec((D, n), fixed), pl.BlockSpec((1, GDN_WIDTH), fixed),
                  pl.BlockSpec((1, GDN_WIDTH), fixed)],
        out_specs=[pl.BlockSpec((1, tm, 3 * GDN_WIDTH), row), pl.BlockSpec((1, tm, GDN_WIDTH), row),
                   pl.BlockSpec((1, tm, GDN_WIDTH), row), pl.BlockSpec((1, tm, GDN_WIDTH), row)],
        out_shape=[jax.ShapeDtypeStruct((B, T, 3 * GDN_WIDTH), F32),
                   jax.ShapeDtypeStruct((B, T, GDN_WIDTH), F32),
                   jax.ShapeDtypeStruct((B, T, GDN_WIDTH), F32),
                   jax.ShapeDtypeStruct((B, T, GDN_WIDTH), F32)],
        compiler_params=_params(("parallel", "parallel")),
        name="gdn_in",
    )(x, gain, w, aneg, dtb)


def _gdn_kernel(qkv_ref, la_ref, beta_ref, gate_ref, cw_ref, gn_ref, out_ref, xbuf, state):
    C = CHUNK
    c = pl.program_id(1)

    @pl.when(c == 0)
    def _():
        xbuf[0:HALO, :] = jnp.zeros((HALO, 3 * GDN_WIDTH), F32)
        state[...] = jnp.zeros(state.shape, F32)

    xbuf[HALO:HALO + C, :] = qkv_ref[0]
    conv = jnp.zeros((C, 3 * GDN_WIDTH), F32)
    for j in range(CONV_K):
        off = HALO - (CONV_K - 1) + j
        conv = conv + cw_ref[j:j + 1, :] * xbuf[off:off + C, :]
    xbuf[0:HALO, :] = xbuf[C:C + HALO, :]
    y = _silu(conv)

    row = _iota2((C, C), 0)
    col = _iota2((C, C), 1)
    causal = col <= row
    strict = col < row
    tri_incl = jnp.where(causal, 1.0, 0.0).astype(BF16)
    upper_incl = jnp.where(row <= col, 1.0, 0.0)
    ones_cc = jnp.ones((C, C), BF16)
    eye = jnp.where(row == col, 1.0, 0.0)
    ones_dd = jnp.ones((GDN_DIM, GDN_DIM), BF16)
    mean_dd = jnp.full((GDN_DIM, GDN_DIM), 1.0 / GDN_DIM, BF16)
    scale = GDN_DIM ** -0.5

    for hd in range(GDN_HEADS):
        sl = slice(hd * GDN_DIM, (hd + 1) * GDN_DIM)
        q = y[:, sl]
        k = y[:, GDN_WIDTH + hd * GDN_DIM:GDN_WIDTH + (hd + 1) * GDN_DIM]
        v = y[:, 2 * GDN_WIDTH + hd * GDN_DIM:2 * GDN_WIDTH + (hd + 1) * GDN_DIM]
        qn = q * lax.rsqrt(_dot_sel_right(q * q, ones_dd) + EPS)
        kn = k * lax.rsqrt(_dot_sel_right(k * k, ones_dd) + EPS)
        la = la_ref[0, :, sl]
        bt = beta_ref[0, :, sl]
        g = _dot_sel_left(tri_incl, la)
        g_row = _dot_sel_left(ones_cc, la[:, :C] * upper_incl)
        diff = g[:, :C] - g_row
        decay = jnp.where(causal, jnp.exp(jnp.where(causal, diff, 0.0)), 0.0)
        eg = jnp.exp(g)
        kb = kn * bt
        a = jnp.where(strict, _dot_nt(kb, kn) * decay, 0.0)
        t_inv = eye
        for lvl in range(1, C.bit_length()):
            same_blk = (row >> lvl) == (col >> lvl)
            lower_left = ((row >> (lvl - 1)) & 1) > ((col >> (lvl - 1)) & 1)
            a_n = jnp.where(same_blk, jnp.where(lower_left, a, 0.0), 0.0)
            t_inv = t_inv - _dot(_dot(t_inv, a_n), t_inv)
        w = _dot(t_inv, kb * eg)
        u = _dot(t_inv, v * bt)
        qs = qn * scale
        qk = jnp.where(causal, _dot_nt(qs, kn) * decay, 0.0)
        qd = qs * eg
        g_last = g[C - 1:C, :]
        kd = kn * jnp.exp(g_last - g)
        s = state[hd]
        v_new = u - _dot(w, s)
        o = _dot(qd, s) + _dot(qk, v_new)
        state[hd] = s * jnp.exp(g_last) + _dot(kd.T, v_new)
        ms = _dot_sel_right(o * o, mean_dd)
        out_ref[0, :, sl] = o * lax.rsqrt(ms + EPS) * gn_ref[...] * gate_ref[0, :, sl]


def _gdn(qkv, la, beta, gate, conv_w, gnorm):
    B, T, _ = qkv.shape
    C = CHUNK
    row = lambda b, c: (b, c, 0)
    fixed = lambda b, c: (0, 0)
    return pl.pallas_call(
        _gdn_kernel,
        grid=(B, T // C),
        in_specs=[pl.BlockSpec((1, C, 3 * GDN_WIDTH), row), pl.BlockSpec((1, C, GDN_WIDTH), row),
                  pl.BlockSpec((1, C, GDN_WIDTH), row), pl.BlockSpec((1, C, GDN_WIDTH), row),
                  pl.BlockSpec((CONV_K, 3 * GDN_WIDTH), fixed), pl.BlockSpec((1, GDN_DIM), fixed)],
        out_specs=pl.BlockSpec((1, C, GDN_WIDTH), row),
        out_shape=jax.ShapeDtypeStruct((B, T, GDN_WIDTH), F32),
        scratch_shapes=[pltpu.VMEM((HALO + C, 3 * GDN_WIDTH), F32),
                        pltpu.VMEM((GDN_HEADS, GDN_DIM, GDN_DIM), F32)],
        compiler_params=_params(("parallel", "arbitrary")),
        name="gdn",
    )(qkv, la, beta, gate, conv_w, gnorm)


def _fox_in_kernel(x_ref, g_ref, wq_ref, wk_ref, wv_ref, wf_ref, fb_ref, gq_ref, gk_ref, pq_ref, pk_ref,
                   cq_ref, ck_ref, q_ref, k_ref, v_ref, cb_ref, carry):
    tm = FOX_BLK
    i = pl.program_id(1)

    @pl.when(i == 0)
    def _():
        carry[...] = jnp.zeros(carry.shape, F32)

    h = _rms_rows(x_ref[0], g_ref[...]).astype(BF16)
    v_ref[0] = jnp.dot(h, wv_ref[...], preferred_element_type=F32).astype(BF16)
    zf = jnp.dot(h, wf_ref[...], preferred_element_type=F32) + fb_ref[...]
    logf = -_softplus(-zf)
    row = _iota2((tm, tm), 0)
    col = _iota2((tm, tm), 1)
    tri_incl = jnp.where(col <= row, 1.0, 0.0).astype(BF16)
    c_loc = _dot_sel_left(tri_incl, logf)
    cb_ref[0, 0] = jnp.broadcast_to(carry[0:1, :], (8, LANES))
    carry[...] = carry[...] + jnp.broadcast_to(c_loc[tm - 1:tm, :], carry.shape)
    hi, lo = _split2(c_loc)
    hl = jnp.concatenate([hi, lo], axis=1)
    mean_sel = jnp.full((LANES, LANES), 1.0 / FOX_DIM, BF16)
    for hd in range(FOX_HEADS):
        sl = slice(hd * LANES, (hd + 1) * LANES)
        for (w_r, gain_r, place_r, const_r, o_r) in ((wq_ref, gq_ref, pq_ref, cq_ref, q_ref),
                                                    (wk_ref, gk_ref, pk_ref, ck_ref, k_ref)):
            p = jnp.dot(h, w_r[:, sl], preferred_element_type=F32)
            ms = _dot_sel_right(p * p, mean_sel)
            pn = p * lax.rsqrt(ms + EPS) * gain_r[...]
            aug = pn + jnp.dot(hl, place_r[hd], preferred_element_type=F32) + const_r[...]
            o_r[0, hd] = aug.astype(BF16)


def _fox_in(x, gain, wq, wk, wv, wf, fb, gq, gk, pq, pk, cq, ck):
    B, T, D = x.shape
    tm = FOX_BLK
    nb = T // tm
    row = lambda b, i: (b, i, 0)
    fixed2 = lambda b, i: (0, 0)
    fixed3 = lambda b, i: (0, 0, 0)
    hrow = lambda b, i: (b, 0, i, 0)
    return pl.pallas_call(
        _fox_in_kernel,
        grid=(B, nb),
        in_specs=[pl.BlockSpec((1, tm, D), row), pl.BlockSpec((1, D), fixed2),
                  pl.BlockSpec((D, FOX_HEADS * LANES), fixed2), pl.BlockSpec((D, FOX_HEADS * LANES), fixed2),
                  pl.BlockSpec((D, FOX_WIDTH), fixed2), pl.BlockSpec((D, LANES), fixed2),
                  pl.BlockSpec((1, LANES), fixed2), pl.BlockSpec((1, LANES), fixed2),
                  pl.BlockSpec((1, LANES), fixed2),
                  pl.BlockSpec((FOX_HEADS, 2 * LANES, LANES), fixed3),
                  pl.BlockSpec((FOX_HEADS, 2 * LANES, LANES), fixed3),
                  pl.BlockSpec((1, LANES), fixed2), pl.BlockSpec((1, LANES), fixed2)],
        out_specs=[pl.BlockSpec((1, FOX_HEADS, tm, LANES), hrow), pl.BlockSpec((1, FOX_HEADS, tm, LANES), hrow),
                   pl.BlockSpec((1, tm, FOX_WIDTH), row),
                   pl.BlockSpec((1, 1, 8, LANES), lambda b, i: (b, i, 0, 0))],
        out_shape=[jax.ShapeDtypeStruct((B, FOX_HEADS, T, LANES), BF16),
                   jax.ShapeDtypeStruct((B, FOX_HEADS, T, LANES), BF16),
                   jax.ShapeDtypeStruct((B, T, FOX_WIDTH), BF16),
                   jax.ShapeDtypeStruct((B, nb, 8, LANES), F32)],
        scratch_shapes=[pltpu.VMEM((8, LANES), F32)],
        compiler_params=_params(("parallel", "arbitrary")),
        name="fox_in",
    )(x, gain, wq, wk, wv, wf, fb, gq, gk, pq, pk, cq, ck)


def _fox_attn_kernel(q_ref, k_ref, v_ref, cb_ref, o_ref, m_s, l_s, acc_s):
    blk = FOX_BLK
    i = pl.program_id(2)
    row = _iota2((blk, blk), 0)
    col = _iota2((blk, blk), 1)
    outs = []
    for hh in range(2):
        q = q_ref[0, hh]
        cbi = cb_ref[0, hh, pl.ds(i, 1), :][:, 0:1]
        m_s[...] = jnp.full(m_s.shape, NEG_INF, F32)
        l_s[...] = jnp.zeros(l_s.shape, F32)
        acc_s[...] = jnp.zeros(acc_s.shape, F32)

        def step(j, masked):
            start = pl.multiple_of(j * blk, blk)
            kj = k_ref[0, hh, pl.ds(start, blk), :]
            vj = v_ref[0, pl.ds(start, blk), :]
            s = _dot_nt(q, kj)
            if masked:
                s = jnp.where(col <= row, s, NEG_INF)
            delta = cbi - cb_ref[0, hh, pl.ds(j, 1), :][:, 0:1]
            m_old = m_s[...]
            m_new = jnp.maximum(m_old, jnp.max(s, axis=-1, keepdims=True) + delta)
            p = jnp.exp(s + (delta - m_new))
            alpha = jnp.exp(m_old - m_new)
            l_s[...] = alpha * l_s[...] + jnp.sum(p, axis=-1, keepdims=True)
            acc_s[...] = alpha * acc_s[...] + jnp.dot(p.astype(BF16), vj, preferred_element_type=F32)
            m_s[...] = m_new

        def body(j, carry):
            step(j, False)
            return carry

        lax.fori_loop(0, i, body, 0)
        step(i, True)
        outs.append(acc_s[...] / l_s[...])
    lane = _iota2((blk, LANES), 1)
    o_ref[0] = jnp.where(lane < FOX_DIM, outs[0], outs[1])


def _fox_attn(q_aug, k_aug, v, cb):
    B, H, T, _ = q_aug.shape
    blk = FOX_BLK
    nb = T // blk
    return pl.pallas_call(
        _fox_attn_kernel,
        grid=(B, H // 2, nb),
        in_specs=[pl.BlockSpec((1, 2, blk, LANES), lambda b, p, i: (b, p, i, 0)),
                  pl.BlockSpec((1, 2, T, LANES), lambda b, p, i: (b, p, 0, 0)),
                  pl.BlockSpec((1, T, LANES), lambda b, p, i: (b, 0, p)),
                  pl.BlockSpec((1, 2, nb, LANES), lambda b, p, i: (b, p, 0, 0))],
        out_specs=pl.BlockSpec((1, blk, LANES), lambda b, p, i: (b, i, p)),
        out_shape=jax.ShapeDtypeStruct((B, T, FOX_WIDTH), F32),
        scratch_shapes=[pltpu.VMEM((blk, 1), F32), pltpu.VMEM((blk, 1), F32), pltpu.VMEM((blk, LANES), F32)],
        compiler_params=_params(("parallel", "parallel", "arbitrary")),
        name="fox_attn",
    )(q_aug, k_aug, v, cb)


def _mem_kv_kernel(mem_ref, g_ref, w_ref, gk_ref, k_ref, v_ref):
    h = _rms_rows(mem_ref[0], g_ref[...]).astype(BF16)
    kv = jnp.dot(h, w_ref[...], preferred_element_type=F32)
    for hd in range(MEM_HEADS):
        kh = kv[:, hd * MEM_DIM:(hd + 1) * MEM_DIM]
        k_ref[0, hd] = _rms_rows(kh, gk_ref[...]).astype(BF16)
        v_ref[0, hd] = kv[:, D_MODEL + hd * MEM_DIM:D_MODEL + (hd + 1) * MEM_DIM].astype(BF16)


def _mem_kv(mem, gain, w, gk):
    B, M, D = mem.shape
    fixed = lambda b: (0, 0)
    return pl.pallas_call(
        _mem_kv_kernel,
        grid=(B,),
        in_specs=[pl.BlockSpec((1, M, D), lambda b: (b, 0, 0)), pl.BlockSpec((1, D), fixed),
                  pl.BlockSpec((D, 2 * D), fixed), pl.BlockSpec((1, MEM_DIM), fixed)],
        out_specs=[pl.BlockSpec((1, MEM_HEADS, M, MEM_DIM), lambda b: (b, 0, 0, 0)),
                   pl.BlockSpec((1, MEM_HEADS, M, MEM_DIM), lambda b: (b, 0, 0, 0))],
        out_shape=[jax.ShapeDtypeStruct((B, MEM_HEADS, M, MEM_DIM), BF16),
                   jax.ShapeDtypeStruct((B, MEM_HEADS, M, MEM_DIM), BF16)],
        compiler_params=_params(("parallel",)),
        name="mem_kv",
    )(mem, gain, w, gk)


def _xattn_kernel(x_ref, og_ref, of_ref, wo_ref, g_ref, wq_ref, gq_ref, k_ref, v_ref, wmo_ref, out_ref):
    x1 = (x_ref[0] + _dot(og_ref[0], wo_ref[:GDN_WIDTH, :]) + _dot(of_ref[0], wo_ref[GDN_WIDTH:, :]))
    h = _rms_rows(x1, g_ref[...]).astype(BF16)
    q = jnp.dot(h, wq_ref[...], preferred_element_type=F32)
    heads = []
    for hd in range(MEM_HEADS):
        sl = slice(hd * MEM_DIM, (hd + 1) * MEM_DIM)
        qn = _rms_rows(q[:, sl], gq_ref[...]) * (MEM_DIM ** -0.5)
        s = _dot_nt(qn, k_ref[0, hd])
        m = jnp.max(s, axis=-1, keepdims=True)
        p = jnp.exp(s - m)
        p = p / jnp.sum(p, axis=-1, keepdims=True)
        heads.append(_dot(p, v_ref[0, hd]).astype(BF16))
    o = jnp.concatenate(heads, axis=1)
    out_ref[0] = x1 + jnp.dot(o, wmo_ref[...], preferred_element_type=F32)


def _xattn(x, og, of, wo, gain, wq, gq, k, v, wmo):
    B, T, D = x.shape
    tm = TOK_TILE
    M = k.shape[2]
    row = lambda b, i: (b, i, 0)
    fixed = lambda b, i: (0, 0)
    kvs = lambda b, i: (b, 0, 0, 0)
    return pl.pallas_call(
        _xattn_kernel,
        grid=(B, T // tm),
        in_specs=[pl.BlockSpec((1, tm, D), row), pl.BlockSpec((1, tm, GDN_WIDTH), row),
                  pl.BlockSpec((1, tm, FOX_WIDTH), row), pl.BlockSpec((D, D), fixed),
                  pl.BlockSpec((1, D), fixed), pl.BlockSpec((D, D), fixed), pl.BlockSpec((1, MEM_DIM), fixed),
                  pl.BlockSpec((1, MEM_HEADS, M, MEM_DIM), kvs), pl.BlockSpec((1, MEM_HEADS, M, MEM_DIM), kvs),
                  pl.BlockSpec((D, D), fixed)],
        out_specs=pl.BlockSpec((1, tm, D), row),
        out_shape=jax.ShapeDtypeStruct((B, T, D), F32),
        compiler_params=_params(("parallel", "parallel")),
        name="xattn",
    )(x, og, of, wo, gain, wq, gq, k, v, wmo)


def _top_values(s, n, rows_ref=None):
    rem = s
    m = None
    for r in range(n):
        m = jnp.max(rem, axis=0, keepdims=True)
        if rows_ref is not None:
            rows_ref[r:r + 1, :] = m
        rem = jnp.where(rem == m, NEG_INF, rem)
    return m


def _peer_kernel(x_ref, g_ref, wqt_ref, keys_ref, u_ref, vt_ref, out_ref, h_s, s_s, e_s, tau_s, top_s, acc_s):
    tm = PEER_TOK
    j = pl.program_id(1)
    nj = pl.num_programs(1)

    @pl.when(j == 0)
    def _():
        h = _rms_rows(x_ref[...], g_ref[...]).astype(BF16)
        h_s[...] = h
        acc_s[...] = jnp.zeros(acc_s.shape, F32)
        for hd in range(PEER_HEADS):
            for half in range(2):
                r0 = (hd * 2 + half) * PEER_HALF
                qt = _dot_nt(wqt_ref[r0:r0 + PEER_HALF, :], h)
                st = _dot(keys_ref[half, hd], qt)
                s_s[half, hd] = st
                _top_values(st, PEER_TOPK, top_s.at[half])
            v2 = top_s[1]
            cand = jnp.concatenate([top_s[0, r:r + 1, :] + v2 for r in range(PEER_TOPK)], axis=0)
            tau = _top_values(cand, PEER_TOPK)
            v1max = top_s[0, 0:1, :]
            v2max = top_s[1, 0:1, :]
            z = jnp.sum(jnp.where(cand >= tau, jnp.exp(cand - (v1max + v2max)), 0.0), axis=0, keepdims=True)
            tau_s[hd] = jnp.broadcast_to(tau, (8, tm))
            e_s[0, hd] = jnp.exp(s_s[0, hd] - v1max)
            e_s[1, hd] = jnp.exp(s_s[1, hd] - v2max) / z

    hb = h_s[...]
    n_sub = PEER_ECHUNK // PEER_SUB
    a_per_sub = PEER_SUB // PEER_KEYS
    for sub in range(n_sub):
        rows = slice(sub * PEER_SUB, (sub + 1) * PEER_SUB)
        act = _gelu_tanh(_dot_nt(u_ref[rows, :], hb))
        gates = []
        for ai in range(a_per_sub):
            a_idx = j * (PEER_ECHUNK // PEER_KEYS) + sub * a_per_sub + ai
            gt = jnp.zeros((PEER_KEYS, tm), F32)
            for hd in range(PEER_HEADS):
                s1 = s_s[0, hd, pl.ds(a_idx, 1), :]
                e1 = e_s[0, hd, pl.ds(a_idx, 1), :]
                sel = (s1 + s_s[1, hd]) >= tau_s[hd, 0:1, :]
                gt = gt + jnp.where(sel, e1 * e_s[1, hd], 0.0)
            gates.append(gt)
        gate = jnp.concatenate(gates, axis=0)
        acc_s[...] += jnp.dot(vt_ref[:, rows], (act * gate).astype(BF16), preferred_element_type=F32)

    @pl.when(j == nj - 1)
    def _():
        out_ref[...] = x_ref[...] + acc_s[...].T


def _peer(x2d, gain, wqt, keys, u, vt):
    M, D = x2d.shape
    tm = PEER_TOK
    E = u.shape[0]
    ec = PEER_ECHUNK
    return pl.pallas_call(
        _peer_kernel,
        grid=(M // tm, E // ec),
        in_specs=[pl.BlockSpec((tm, D), lambda i, j: (i, 0)), pl.BlockSpec((1, D), lambda i, j: (0, 0)),
                  pl.BlockSpec((PEER_HEADS * 2 * PEER_HALF, D), lambda i, j: (0, 0)),
                  pl.BlockSpec((2, PEER_HEADS, PEER_KEYS, PEER_HALF), lambda i, j: (0, 0, 0, 0)),
                  pl.BlockSpec((ec, D), lambda i, j: (j, 0)), pl.BlockSpec((D, ec), lambda i, j: (0, j))],
        out_specs=pl.BlockSpec((tm, D), lambda i, j: (i, 0)),
        out_shape=jax.ShapeDtypeStruct((M, D), F32),
        scratch_shapes=[pltpu.VMEM((tm, D), BF16),
                        pltpu.VMEM((2, PEER_HEADS, PEER_KEYS, tm), F32),
                        pltpu.VMEM((2, PEER_HEADS, PEER_KEYS, tm), F32),
                        pltpu.VMEM((PEER_HEADS, 8, tm), F32),
                        pltpu.VMEM((2, PEER_TOPK, tm), F32),
                        pltpu.VMEM((D, tm), F32)],
        compiler_params=_params(("parallel", "arbitrary")),
        name="peer",
    )(x2d, gain, wqt, keys, u, vt)


def _lane_rep(v, width):
    return jnp.repeat(v.astype(F32), width)[None, :]


def _fox_layout(w_heads):
    D = w_heads.shape[0]
    out = jnp.zeros((D, FOX_HEADS, LANES), F32)
    out = out.at[:, :, :FOX_DIM].set(w_heads.reshape(D, FOX_HEADS, FOX_DIM))
    return out.reshape(D, FOX_HEADS * LANES).astype(BF16)


def _placement(lane_hi, lane_lo, sign):
    p = jnp.zeros((FOX_HEADS, 2 * LANES, LANES), F32)
    hs = jnp.arange(FOX_HEADS)
    p = p.at[hs, hs, lane_hi].set(sign)
    p = p.at[hs, LANES + hs, lane_lo].set(sign)
    return p.astype(BF16)


def _lane_const(lanes):
    c = jnp.zeros((1, LANES), F32)
    return c.at[0, jnp.array(lanes)].set(1.0)


def _pad_lanes(v, scale=1.0):
    return jnp.zeros((1, LANES), F32).at[0, :v.shape[0]].set(v.astype(F32) * scale)


def kernel(x, mem, mix_norm, w_in, conv_w, a_log, dt_bias, gdn_out_norm, fox_q_norm, fox_k_norm, fox_f_bias,
           w_out, xattn_norm, mem_norm, w_mq, w_mkv, mq_norm, mk_norm, w_mo, ffn_norm, w_pq, sub_keys,
           expert_u, expert_v):
    B, T, D = x.shape
    depth = w_in.shape[0]
    o1 = 3 * GDN_WIDTH
    o2 = o1 + GDN_WIDTH
    o3 = o2 + GDN_HEADS
    o4 = o3 + GDN_HEADS
    o5 = o4 + 3 * FOX_WIDTH
    nb = T // FOX_BLK
    place_q = _placement(FOX_DIM, FOX_DIM + 1, 1.0)
    place_k = _placement(FOX_DIM + 2, FOX_DIM + 3, -1.0)
    const_q = _lane_const([FOX_DIM + 2, FOX_DIM + 3])
    const_k = _lane_const([FOX_DIM, FOX_DIM + 1])

    for l in range(depth):
        wi = w_in[l]
        w_gdn = jnp.concatenate([wi[:, :o2], jnp.repeat(wi[:, o2:o3], GDN_DIM, axis=1),
                                 jnp.repeat(wi[:, o3:o4], GDN_DIM, axis=1)], axis=1).astype(BF16)
        aneg = _lane_rep(-jnp.exp(a_log[l].astype(F32)), GDN_DIM)
        dtb = _lane_rep(dt_bias[l], GDN_DIM)
        qkv, gate, la, beta = _gdn_in(x, mix_norm[l][None, :], w_gdn, aneg, dtb)
        o_gdn = _gdn(qkv, la, beta, gate, conv_w[l].reshape(CONV_K, 3 * GDN_WIDTH),
                     gdn_out_norm[l][None, :])
        fq_w = _fox_layout(wi[:, o4:o4 + FOX_WIDTH])
        fk_w = _fox_layout(wi[:, o4 + FOX_WIDTH:o4 + 2 * FOX_WIDTH])
        fv_w = wi[:, o4 + 2 * FOX_WIDTH:o5].astype(BF16)
        ff_w = jnp.zeros((D, LANES), F32).at[:, :FOX_HEADS].set(wi[:, o5:]).astype(BF16)
        q_aug, k_aug, v_fox, cb = _fox_in(
            x, mix_norm[l][None, :], fq_w, fk_w, fv_w, ff_w, _pad_lanes(fox_f_bias[l]),
            _pad_lanes(fox_q_norm[l], FOX_DIM ** -0.5), _pad_lanes(fox_k_norm[l]),
            place_q, place_k, const_q, const_k)
        cb_heads = jnp.broadcast_to(
            jnp.transpose(cb[:, :, 0, :FOX_HEADS], (0, 2, 1))[..., None], (B, FOX_HEADS, nb, LANES))
        o_fox = _fox_attn(q_aug, k_aug, v_fox, cb_heads)
        k_mem, v_mem = _mem_kv(mem, mem_norm[l][None, :], w_mkv[l].astype(BF16), mk_norm[l][None, :])
        x = _xattn(x, o_gdn, o_fox, w_out[l].astype(BF16), xattn_norm[l][None, :], w_mq[l].astype(BF16),
                   mq_norm[l][None, :], k_mem, v_mem, w_mo[l].astype(BF16))
        x = _peer(x.reshape(B * T, D), ffn_norm[l][None, :], w_pq[l].T.astype(BF16),
                  sub_keys[l].astype(BF16), expert_u[l].astype(BF16),
                  expert_v[l].T.astype(BF16)).reshape(B, T, D)
    return x
```

```python
import functools

import jax
import jax.numpy as jnp
from jax import lax
from jax.experimental import pallas as pl
from jax.experimental.pallas import tpu as pltpu

F32 = jnp.float32
BF16 = jnp.bfloat16
EPS = 1e-6
NEG_INF = float("-inf")
LOG2E = 1.4426950408889634

D_MODEL = 1024
CHUNK = 64
GDN_HEADS = 4
GDN_DIM = 128
GDN_WIDTH = GDN_HEADS * GDN_DIM
CONV_K = 4
FOX_HEADS = 8
FOX_DIM = 64
FOX_WIDTH = FOX_HEADS * FOX_DIM
MEM_HEADS = 4
MEM_DIM = 256
PEER_HEADS = 8
PEER_KEYS = 128
PEER_HALF = 128
PEER_TOPK = 16
PEER_EXPERTS = PEER_KEYS * PEER_KEYS
LANES = 128
HALO = 8
GDN_NCH = 4

TOK_TILE = 512
FOX_BLK = 512
PEER_TOK = 512
PEER_ECHUNK = 2048
PEER_SUB = 1024
VMEM_LIMIT = 56 * 1024 * 1024


def _dot(a, b):
    return jnp.dot(a.astype(BF16), b.astype(BF16), preferred_element_type=F32)


def _dot_nt(a, b):
    return lax.dot_general(a.astype(BF16), b.astype(BF16), (((1,), (1,)), ((), ())),
                           preferred_element_type=F32)


def _split2(x):
    hi = x.astype(BF16)
    lo = (x - hi.astype(F32)).astype(BF16)
    return hi, lo


def _split3(x):
    hi = x.astype(BF16)
    r = x - hi.astype(F32)
    mid = r.astype(BF16)
    lo = (r - mid.astype(F32)).astype(BF16)
    return hi, mid, lo


def _dot_sel_left(sel, x):
    hi, mid, lo = _split3(x)
    d = functools.partial(jnp.dot, preferred_element_type=F32)
    return d(sel, hi) + d(sel, mid) + d(sel, lo)


def _dot_sel_right(x, sel, pieces=2):
    d = functools.partial(jnp.dot, preferred_element_type=F32)
    if pieces == 2:
        hi, lo = _split2(x)
        return d(hi, sel) + d(lo, sel)
    hi, mid, lo = _split3(x)
    return d(hi, sel) + d(mid, sel) + d(lo, sel)


def _sigmoid(x):
    return 1.0 / (1.0 + jnp.exp(-x))


def _silu(x):
    return x * _sigmoid(x)


def _softplus(x):
    return jnp.maximum(x, 0.0) + jnp.log(1.0 + jnp.exp(-jnp.abs(x)))


def _gelu_tanh(x):
    c = 0.7978845608028654
    return 0.5 * x * (1.0 + jnp.tanh(c * (x + 0.044715 * (x * x * x))))


def _rms_rows(x, gain):
    ms = jnp.mean(x * x, axis=-1, keepdims=True)
    return x * lax.rsqrt(ms + EPS) * gain


def _iota2(shape, dim):
    return lax.broadcasted_iota(jnp.int32, shape, dim)


def _params(sem):
    return pltpu.CompilerParams(dimension_semantics=sem, vmem_limit_bytes=VMEM_LIMIT)


def _gdn_in_kernel(x_ref, g_ref, w_ref, aneg_ref, dtb_ref, qkv_ref, gate_ref, la_ref, beta_ref):
    h = _rms_rows(x_ref[0], g_ref[...]).astype(BF16)
    o_gate = 3 * GDN_WIDTH
    o_a = o_gate + GDN_WIDTH
    o_b = o_a + GDN_WIDTH
    qkv_ref[0] = jnp.dot(h, w_ref[:, :o_gate], preferred_element_type=F32)
    gate_ref[0] = _silu(jnp.dot(h, w_ref[:, o_gate:o_a], preferred_element_type=F32))
    a = jnp.dot(h, w_ref[:, o_a:o_b], preferred_element_type=F32)
    la_ref[0] = aneg_ref[...] * _softplus(a + dtb_ref[...])
    b = jnp.dot(h, w_ref[:, o_b:], preferred_element_type=F32)
    beta_ref[0] = _sigmoid(b)


def _gdn_in(x, gain, w, aneg, dtb):
    B, T, D = x.shape
    n = w.shape[1]
    tm = TOK_TILE
    row = lambda b, i: (b, i, 0)
    fixed = lambda b, i: (0, 0)
    return pl.pallas_call(
        _gdn_in_kernel,
        grid=(B, T // tm),
        in_specs=[pl.BlockSpec((1, tm, D), row), pl.BlockSpec((1, D), fixed),
                  pl.BlockSpec((D, n), fixed), pl.BlockSpec((1, GDN_WIDTH), fixed),
                  pl.BlockSpec((1, GDN_WIDTH), fixed)],
        out_specs=[pl.BlockSpec((1, tm, 3 * GDN_WIDTH), row), pl.BlockSpec((1, tm, GDN_WIDTH), row),
                   pl.BlockSpec((1, tm, GDN_WIDTH), row), pl.BlockSpec((1, tm, GDN_WIDTH), row)],
        out_shape=[jax.ShapeDtypeStruct((B, T, 3 * GDN_WIDTH), F32),
                   jax.ShapeDtypeStruct((B, T, GDN_WIDTH), F32),
                   jax.ShapeDtypeStruct((B, T, GDN_WIDTH), F32),
                   jax.ShapeDtypeStruct((B, T, GDN_WIDTH), F32)],
        compiler_params=_params(("parallel", "parallel")),
        name="gdn_in",
    )(x, gain, w, aneg, dtb)


def _bdot(a, b):
    return jnp.einsum("uik,ukj->uij", a.astype(BF16), b.astype(BF16), preferred_element_type=F32)


def _bdot_nt(a, b):
    return jnp.einsum("uik,ujk->uij", a.astype(BF16), b.astype(BF16), preferred_element_type=F32)


def _gdn_kernel(qkv_ref, la_ref, beta_ref, gate_ref, cw_ref, gn_ref, bd_ref, btri_ref, out_ref, xbuf, state):
    C = CHUNK
    N = GDN_NCH
    H = GDN_HEADS
    R = N * C

    @pl.when(pl.program_id(1) == 0)
    def _():
        xbuf[0:HALO, :] = jnp.zeros((HALO, 3 * GDN_WIDTH), F32)
        state[...] = jnp.zeros(state.shape, F32)

    xbuf[HALO:HALO + R, :] = qkv_ref[0]
    conv = jnp.zeros((R, 3 * GDN_WIDTH), F32)
    for j in range(CONV_K):
        off = HALO - (CONV_K - 1) + j
        conv = conv + cw_ref[j:j + 1, :] * xbuf[off:off + R, :]
    xbuf[0:HALO, :] = xbuf[R:R + HALO, :]
    y = _silu(conv)

    bd = bd_ref[...]
    q_all = y[:, :GDN_WIDTH]
    k_all = y[:, GDN_WIDTH:2 * GDN_WIDTH]
    qs_all = q_all * lax.rsqrt(_dot_sel_right(q_all * q_all, bd) + EPS) * (GDN_DIM ** -0.5)
    kn_all = k_all * lax.rsqrt(_dot_sel_right(k_all * k_all, bd) + EPS)
    g_all = _dot_sel_left(btri_ref[...], la_ref[0])

    def units(arr, col0=0):
        return jnp.stack([arr[c * C:(c + 1) * C, col0 + h * GDN_DIM:col0 + (h + 1) * GDN_DIM]
                          for c in range(N) for h in range(H)])

    qs = units(qs_all)
    kn = units(kn_all)
    v = units(y, 2 * GDN_WIDTH)
    bt = units(beta_ref[0])
    g = units(g_all)

    row = _iota2((C, C), 0)
    col = _iota2((C, C), 1)
    causal = (col <= row)[None]
    strict = (col < row)[None]
    eye = jnp.where(row == col, 1.0, 0.0)[None]
    g_col = g[:, :, :C]
    diff = g_col - jnp.swapaxes(g_col, 1, 2)
    decay = jnp.where(causal, jnp.exp(jnp.where(causal, diff, 0.0)), 0.0)
    eg = jnp.exp(g)
    kb = kn * bt
    a = jnp.where(strict, _bdot_nt(kb, kn) * decay, 0.0)
    t_inv = None
    for lvl in range(1, C.bit_length()):
        same_blk = (row >> lvl) == (col >> lvl)
        lower_left = ((row >> (lvl - 1)) & 1) > ((col >> (lvl - 1)) & 1)
        a_n = jnp.where((same_blk & lower_left)[None], a, 0.0)
        t_inv = eye - a_n if t_inv is None else t_inv - _bdot(_bdot(t_inv, a_n), t_inv)
    w = _bdot(t_inv, kb * eg)
    u_all = _bdot(t_inv, v * bt)
    qk = jnp.where(causal, _bdot_nt(qs, kn) * decay, 0.0)
    qd = qs * eg
    g_last = g[:, C - 1:C, :]
    kd_t = jnp.swapaxes(kn * jnp.exp(g_last - g), 1, 2)
    eg_last = jnp.exp(g_last)

    s = [state[h] for h in range(H)]
    o_rows = []
    for c in range(N):
        o_heads = []
        for h in range(H):
            u = c * H + h
            v_new = u_all[u] - _dot(w[u], s[h])
            o_heads.append(_dot(qd[u], s[h]) + _dot(qk[u], v_new))
            s[h] = s[h] * eg_last[u] + _dot(kd_t[u], v_new)
        o_rows.append(jnp.concatenate(o_heads, axis=1))
    for h in range(H):
        state[h] = s[h]
    o = jnp.concatenate(o_rows, axis=0)
    ms = _dot_sel_right(o * o, bd) * (1.0 / GDN_DIM)
    out_ref[0] = o * lax.rsqrt(ms + EPS) * gn_ref[...] * gate_ref[0]


def _gdn(qkv, la, beta, gate, conv_w, gnorm):
    B, T, _ = qkv.shape
    R = GDN_NCH * CHUNK
    row = lambda b, c: (b, c, 0)
    fixed = lambda b, c: (0, 0)
    idx = jnp.arange(GDN_WIDTH)
    bd = (idx[:, None] // GDN_DIM == idx[None, :] // GDN_DIM).astype(BF16)
    r = jnp.arange(R)
    btri = ((r[:, None] // CHUNK == r[None, :] // CHUNK) & (r[None, :] <= r[:, None])).astype(BF16)
    return pl.pallas_call(
        _gdn_kernel,
        grid=(B, T // R),
        in_specs=[pl.BlockSpec((1, R, 3 * GDN_WIDTH), row), pl.BlockSpec((1, R, GDN_WIDTH), row),
                  pl.BlockSpec((1, R, GDN_WIDTH), row), pl.BlockSpec((1, R, GDN_WIDTH), row),
                  pl.BlockSpec((CONV_K, 3 * GDN_WIDTH), fixed), pl.BlockSpec((1, GDN_WIDTH), fixed),
                  pl.BlockSpec((GDN_WIDTH, GDN_WIDTH), fixed), pl.BlockSpec((R, R), fixed)],
        out_specs=pl.BlockSpec((1, R, GDN_WIDTH), row),
        out_shape=jax.ShapeDtypeStruct((B, T, GDN_WIDTH), F32),
        scratch_shapes=[pltpu.VMEM((HALO + R, 3 * GDN_WIDTH), F32),
                        pltpu.VMEM((GDN_HEADS, GDN_DIM, GDN_DIM), F32)],
        compiler_params=_params(("parallel", "arbitrary")),
        name="gdn",
    )(qkv, la, beta, gate, conv_w, jnp.tile(gnorm.astype(F32), GDN_HEADS)[None, :], bd, btri)


def _fox_in_kernel(x_ref, g_ref, wq_ref, wk_ref, wv_ref, wf_ref, fb_ref, gq_ref, gk_ref, pq_ref, pk_ref,
                   cq_ref, ck_ref, q_ref, k_ref, v_ref, cb_ref, carry):
    tm = FOX_BLK
    i = pl.program_id(1)

    @pl.when(i == 0)
    def _():
        carry[...] = jnp.zeros(carry.shape, F32)

    h = _rms_rows(x_ref[0], g_ref[...]).astype(BF16)
    v_ref[0] = jnp.dot(h, wv_ref[...], preferred_element_type=F32).astype(BF16)
    zf = jnp.dot(h, wf_ref[...], preferred_element_type=F32) + fb_ref[...]
    logf = -_softplus(-zf)
    row = _iota2((tm, tm), 0)
    col = _iota2((tm, tm), 1)
    tri_incl = jnp.where(col <= row, 1.0, 0.0).astype(BF16)
    c_loc = _dot_sel_left(tri_incl, logf * LOG2E)
    cb_ref[0, 0] = jnp.broadcast_to(carry[0:1, :], (8, LANES))
    carry[...] = carry[...] + jnp.broadcast_to(c_loc[tm - 1:tm, :], carry.shape)
    hi, lo = _split2(c_loc)
    hl = jnp.concatenate([hi, lo], axis=1)
    mean_sel = jnp.full((LANES, LANES), 1.0 / FOX_DIM, BF16)
    for hd in range(FOX_HEADS):
        sl = slice(hd * LANES, (hd + 1) * LANES)
        for (w_r, gain_r, place_r, const_r, o_r) in ((wq_ref, gq_ref, pq_ref, cq_ref, q_ref),
                                                    (wk_ref, gk_ref, pk_ref, ck_ref, k_ref)):
            p = jnp.dot(h, w_r[:, sl], preferred_element_type=F32)
            ms = _dot_sel_right(p * p, mean_sel)
            pn = p * lax.rsqrt(ms + EPS) * gain_r[...]
            aug = pn + jnp.dot(hl, place_r[hd], preferred_element_type=F32) + const_r[...]
            o_r[0, hd] = aug.astype(BF16)


def _fox_in(x, gain, wq, wk, wv, wf, fb, gq, gk, pq, pk, cq, ck):
    B, T, D = x.shape
    tm = FOX_BLK
    nb = T // tm
    row = lambda b, i: (b, i, 0)
    fixed2 = lambda b, i: (0, 0)
    fixed3 = lambda b, i: (0, 0, 0)
    hrow = lambda b, i: (b, 0, i, 0)
    return pl.pallas_call(
        _fox_in_kernel,
        grid=(B, nb),
        in_specs=[pl.BlockSpec((1, tm, D), row), pl.BlockSpec((1, D), fixed2),
                  pl.BlockSpec((D, FOX_HEADS * LANES), fixed2), pl.BlockSpec((D, FOX_HEADS * LANES), fixed2),
                  pl.BlockSpec((D, FOX_WIDTH), fixed2), pl.BlockSpec((D, LANES), fixed2),
                  pl.BlockSpec((1, LANES), fixed2), pl.BlockSpec((1, LANES), fixed2),
                  pl.BlockSpec((1, LANES), fixed2),
                  pl.BlockSpec((FOX_HEADS, 2 * LANES, LANES), fixed3),
                  pl.BlockSpec((FOX_HEADS, 2 * LANES, LANES), fixed3),
                  pl.BlockSpec((1, LANES), fixed2), pl.BlockSpec((1, LANES), fixed2)],
        out_specs=[pl.BlockSpec((1, FOX_HEADS, tm, LANES), hrow), pl.BlockSpec((1, FOX_HEADS, tm, LANES), hrow),
                   pl.BlockSpec((1, tm, FOX_WIDTH), row),
                   pl.BlockSpec((1, 1, 8, LANES), lambda b, i: (b, i, 0, 0))],
        out_shape=[jax.ShapeDtypeStruct((B, FOX_HEADS, T, LANES), BF16),
                   jax.ShapeDtypeStruct((B, FOX_HEADS, T, LANES), BF16),
                   jax.ShapeDtypeStruct((B, T, FOX_WIDTH), BF16),
                   jax.ShapeDtypeStruct((B, nb, 8, LANES), F32)],
        scratch_shapes=[pltpu.VMEM((8, LANES), F32)],
        compiler_params=_params(("parallel", "arbitrary")),
        name="fox_in",
    )(x, gain, wq, wk, wv, wf, fb, gq, gk, pq, pk, cq, ck)


def _fox_attn_kernel(q_ref, k_ref, v_ref, cb_ref, o_ref, m_s, l_s, acc_s):
    blk = FOX_BLK
    reps = blk // LANES
    i = pl.program_id(2)
    m_s[...] = jnp.full(m_s.shape, NEG_INF, F32)
    l_s[...] = jnp.zeros(l_s.shape, F32)
    acc_s[...] = jnp.zeros(acc_s.shape, F32)

    def step(j, masked):
        start = pl.multiple_of(j * blk, blk)
        vj = v_ref[0, pl.ds(start, blk), :]
        for hh in range(2):
            kj = k_ref[0, hh, pl.ds(start, blk), :]
            s = _dot_nt(q_ref[0, hh], kj)
            if masked:
                s = jnp.where(_iota2((blk, blk), 1) <= _iota2((blk, blk), 0), s, NEG_INF)
            delta = cb_ref[0, hh, pl.ds(i, 1), :] - cb_ref[0, hh, pl.ds(j, 1), :]
            m_old = m_s[hh]
            m_new = jnp.maximum(m_old, jnp.max(s, axis=-1, keepdims=True) + delta)
            shift = delta - m_new
            p = jnp.exp2(s + jnp.concatenate([shift] * reps, axis=1))
            alpha = jnp.exp2(m_old - m_new)
            l_s[hh] = alpha * l_s[hh] + jnp.sum(p, axis=-1, keepdims=True)
            acc_s[hh] = alpha * acc_s[hh] + jnp.dot(p.astype(BF16), vj, preferred_element_type=F32)
            m_s[hh] = m_new

    def body(j, carry):
        step(j, False)
        return carry

    lax.fori_loop(0, i, body, 0)
    step(i, True)
    lane = _iota2((blk, LANES), 1)
    o_ref[0] = jnp.where(lane < FOX_DIM, acc_s[0] / l_s[0], acc_s[1] / l_s[1])


def _fox_attn(q_aug, k_aug, v, cb):
    B, H, T, _ = q_aug.shape
    blk = FOX_BLK
    nb = T // blk
    return pl.pallas_call(
        _fox_attn_kernel,
        grid=(B, H // 2, nb),
        in_specs=[pl.BlockSpec((1, 2, blk, LANES), lambda b, p, i: (b, p, i, 0)),
                  pl.BlockSpec((1, 2, T, LANES), lambda b, p, i: (b, p, 0, 0)),
                  pl.BlockSpec((1, T, LANES), lambda b, p, i: (b, 0, p)),
                  pl.BlockSpec((1, 2, nb, LANES), lambda b, p, i: (b, p, 0, 0))],
        out_specs=pl.BlockSpec((1, blk, LANES), lambda b, p, i: (b, i, p)),
        out_shape=jax.ShapeDtypeStruct((B, T, FOX_WIDTH), F32),
        scratch_shapes=[pltpu.VMEM((2, blk, LANES), F32), pltpu.VMEM((2, blk, LANES), F32),
                        pltpu.VMEM((2, blk, LANES), F32)],
        compiler_params=_params(("parallel", "parallel", "arbitrary")),
        name="fox_attn",
    )(q_aug, k_aug, v, cb)


def _mem_kv_kernel(mem_ref, g_ref, w_ref, gk_ref, k_ref, v_ref):
    h = _rms_rows(mem_ref[0], g_ref[...]).astype(BF16)
    kv = jnp.dot(h, w_ref[...], preferred_element_type=F32)
    for hd in range(MEM_HEADS):
        kh = kv[:, hd * MEM_DIM:(hd + 1) * MEM_DIM]
        k_ref[0, hd] = _rms_rows(kh, gk_ref[...]).astype(BF16)
        v_ref[0, hd] = kv[:, D_MODEL + hd * MEM_DIM:D_MODEL + (hd + 1) * MEM_DIM].astype(BF16)


def _mem_kv(mem, gain, w, gk):
    B, M, D = mem.shape
    fixed = lambda b: (0, 0)
    return pl.pallas_call(
        _mem_kv_kernel,
        grid=(B,),
        in_specs=[pl.BlockSpec((1, M, D), lambda b: (b, 0, 0)), pl.BlockSpec((1, D), fixed),
                  pl.BlockSpec((D, 2 * D), fixed), pl.BlockSpec((1, MEM_DIM), fixed)],
        out_specs=[pl.BlockSpec((1, MEM_HEADS, M, MEM_DIM), lambda b: (b, 0, 0, 0)),
                   pl.BlockSpec((1, MEM_HEADS, M, MEM_DIM), lambda b: (b, 0, 0, 0))],
        out_shape=[jax.ShapeDtypeStruct((B, MEM_HEADS, M, MEM_DIM), BF16),
                   jax.ShapeDtypeStruct((B, MEM_HEADS, M, MEM_DIM), BF16)],
        compiler_params=_params(("parallel",)),
        name="mem_kv",
    )(mem, gain, w, gk)


def _xattn_kernel(x_ref, og_ref, of_ref, wo_ref, g_ref, wq_ref, gq_ref, k_ref, v_ref, wmo_ref, out_ref):
    x1 = (x_ref[0] + _dot(og_ref[0], wo_ref[:GDN_WIDTH, :]) + _dot(of_ref[0], wo_ref[GDN_WIDTH:, :]))
    h = _rms_rows(x1, g_ref[...]).astype(BF16)
    q = jnp.dot(h, wq_ref[...], preferred_element_type=F32)
    heads = []
    for hd in range(MEM_HEADS):
        sl = slice(hd * MEM_DIM, (hd + 1) * MEM_DIM)
        qn = _rms_rows(q[:, sl], gq_ref[...]) * (MEM_DIM ** -0.5)
        s = _dot_nt(qn, k_ref[0, hd])
        m = jnp.max(s, axis=-1, keepdims=True)
        p = jnp.exp(s - m)
        p = p / jnp.sum(p, axis=-1, keepdims=True)
        heads.append(_dot(p, v_ref[0, hd]).astype(BF16))
    o = jnp.concatenate(heads, axis=1)
    out_ref[0] = x1 + jnp.dot(o, wmo_ref[...], preferred_element_type=F32)


def _xattn(x, og, of, wo, gain, wq, gq, k, v, wmo):
    B, T, D = x.shape
    tm = TOK_TILE
    M = k.shape[2]
    row = lambda b, i: (b, i, 0)
    fixed = lambda b, i: (0, 0)
    kvs = lambda b, i: (b, 0, 0, 0)
    return pl.pallas_call(
        _xattn_kernel,
        grid=(B, T // tm),
        in_specs=[pl.BlockSpec((1, tm, D), row), pl.BlockSpec((1, tm, GDN_WIDTH), row),
                  pl.BlockSpec((1, tm, FOX_WIDTH), row), pl.BlockSpec((D, D), fixed),
                  pl.BlockSpec((1, D), fixed), pl.BlockSpec((D, D), fixed), pl.BlockSpec((1, MEM_DIM), fixed),
                  pl.BlockSpec((1, MEM_HEADS, M, MEM_DIM), kvs), pl.BlockSpec((1, MEM_HEADS, M, MEM_DIM), kvs),
                  pl.BlockSpec((D, D), fixed)],
        out_specs=pl.BlockSpec((1, tm, D), row),
        out_shape=jax.ShapeDtypeStruct((B, T, D), F32),
        compiler_params=_params(("parallel", "parallel")),
        name="xattn",
    )(x, og, of, wo, gain, wq, gq, k, v, wmo)


def _top_values(s, n, rows_ref):
    m = jnp.max(s, axis=0, keepdims=True)
    rows_ref[0:1, :] = m
    for r in range(1, n):
        m = jnp.max(jnp.where(s < m, s, NEG_INF), axis=0, keepdims=True)
        rows_ref[r:r + 1, :] = m


PEER_NSEL = PEER_TOPK + 1
PEER_PAIRS = [(i, j) for i in range(PEER_NSEL) for j in range(PEER_NSEL) if (i + 1) * (j + 1) <= PEER_NSEL]
PEER_CAND_ROWS = -(-len(PEER_PAIRS) // 8) * 8
PEER_TOP_ROWS = -(-PEER_NSEL // 8) * 8


def _peer_kernel(x_ref, g_ref, wqt_ref, keys_ref, u_ref, vt_ref, out_ref, h_s, s_s, e_s, top_s, cand_s, acc_s):
    tm = PEER_TOK
    j = pl.program_id(1)
    nj = pl.num_programs(1)

    @pl.when(j == 0)
    def _():
        h = _rms_rows(x_ref[...], g_ref[...]).astype(BF16)
        h_s[...] = h
        acc_s[...] = jnp.zeros(acc_s.shape, F32)
        cand_s[...] = jnp.full(cand_s.shape, NEG_INF, F32)
        for hd in range(PEER_HEADS):
            for half in range(2):
                r0 = (hd * 2 + half) * PEER_HALF
                qt = _dot_nt(wqt_ref[r0:r0 + PEER_HALF, :], h)
                st = _dot(keys_ref[half, hd], qt)
                s_s[half, hd] = st
                _top_values(st, PEER_NSEL, top_s.at[half])
            for r, (i1, i2) in enumerate(PEER_PAIRS):
                cand_s[r:r + 1, :] = top_s[0, i1:i1 + 1, :] + top_s[1, i2:i2 + 1, :]
            cand = cand_s[...]
            _top_values(cand, PEER_NSEL, top_s.at[2])
            tau = 0.5 * (top_s[2, PEER_TOPK - 1:PEER_TOPK, :] + top_s[2, PEER_TOPK:PEER_TOPK + 1, :])
            v1max = top_s[0, 0:1, :]
            v2max = top_s[1, 0:1, :]
            z = jnp.sum(jnp.where(cand >= tau, jnp.exp(cand - (v1max + v2max)), 0.0), axis=0, keepdims=True)
            e_s[0, hd] = jnp.exp(s_s[0, hd] - v1max)
            e_s[1, hd] = jnp.exp(s_s[1, hd] - v2max) / z
            s_s[1, hd] = tau - s_s[1, hd]

    hb = h_s[...]
    n_sub = PEER_ECHUNK // PEER_SUB
    a_per_sub = PEER_SUB // PEER_KEYS
    for sub in range(n_sub):
        rows = slice(sub * PEER_SUB, (sub + 1) * PEER_SUB)
        scores = _dot_nt(u_ref[rows, :], hb)
        gates = []
        for ai in range(a_per_sub):
            a_idx = j * (PEER_ECHUNK // PEER_KEYS) + sub * a_per_sub + ai
            gt = None
            for hd in range(PEER_HEADS):
                s1 = s_s[0, hd, pl.ds(a_idx, 1), :]
                e1 = e_s[0, hd, pl.ds(a_idx, 1), :]
                term = jnp.where(s1 >= s_s[1, hd], e1 * e_s[1, hd], 0.0)
                gt = term if gt is None else gt + term
            gates.append(gt)
        gate = jnp.concatenate(gates, axis=0)
        weighted = (_gelu_tanh(scores) * gate).astype(BF16)
        acc_s[...] += jnp.dot(vt_ref[:, rows], weighted, preferred_element_type=F32)

    @pl.when(j == nj - 1)
    def _():
        out_ref[...] = x_ref[...] + acc_s[...].T


def _peer(x2d, gain, wqt, keys, u, vt):
    M, D = x2d.shape
    tm = PEER_TOK
    E = u.shape[0]
    ec = PEER_ECHUNK
    return pl.pallas_call(
        _peer_kernel,
        grid=(M // tm, E // ec),
        in_specs=[pl.BlockSpec((tm, D), lambda i, j: (i, 0)), pl.BlockSpec((1, D), lambda i, j: (0, 0)),
                  pl.BlockSpec((PEER_HEADS * 2 * PEER_HALF, D), lambda i, j: (0, 0)),
                  pl.BlockSpec((2, PEER_HEADS, PEER_KEYS, PEER_HALF), lambda i, j: (0, 0, 0, 0)),
                  pl.BlockSpec((ec, D), lambda i, j: (j, 0)), pl.BlockSpec((D, ec), lambda i, j: (0, j))],
        out_specs=pl.BlockSpec((tm, D), lambda i, j: (i, 0)),
        out_shape=jax.ShapeDtypeStruct((M, D), F32),
        scratch_shapes=[pltpu.VMEM((tm, D), BF16),
                        pltpu.VMEM((2, PEER_HEADS, PEER_KEYS, tm), F32),
                        pltpu.VMEM((2, PEER_HEADS, PEER_KEYS, tm), F32),
                        pltpu.VMEM((3, PEER_TOP_ROWS, tm), F32),
                        pltpu.VMEM((PEER_CAND_ROWS, tm), F32),
                        pltpu.VMEM((D, tm), F32)],
        compiler_params=_params(("parallel", "arbitrary")),
        name="peer",
    )(x2d, gain, wqt, keys, u, vt)


def _lane_rep(v, width):
    return jnp.repeat(v.astype(F32), width)[None, :]


def _fox_layout(w_heads):
    D = w_heads.shape[0]
    out = jnp.zeros((D, FOX_HEADS, LANES), F32)
    out = out.at[:, :, :FOX_DIM].set(w_heads.reshape(D, FOX_HEADS, FOX_DIM))
    return out.reshape(D, FOX_HEADS * LANES).astype(BF16)


def _placement(lane_hi, lane_lo, sign):
    p = jnp.zeros((FOX_HEADS, 2 * LANES, LANES), F32)
    hs = jnp.arange(FOX_HEADS)
    p = p.at[hs, hs, lane_hi].set(sign)
    p = p.at[hs, LANES + hs, lane_lo].set(sign)
    return p.astype(BF16)


def _lane_const(lanes):
    c = jnp.zeros((1, LANES), F32)
    return c.at[0, jnp.array(lanes)].set(1.0)


def _pad_lanes(v, scale=1.0):
    return jnp.zeros((1, LANES), F32).at[0, :v.shape[0]].set(v.astype(F32) * scale)


def kernel(x, mem, mix_norm, w_in, conv_w, a_log, dt_bias, gdn_out_norm, fox_q_norm, fox_k_norm, fox_f_bias,
           w_out, xattn_norm, mem_norm, w_mq, w_mkv, mq_norm, mk_norm, w_mo, ffn_norm, w_pq, sub_keys,
           expert_u, expert_v):
    B, T, D = x.shape
    depth = w_in.shape[0]
    o1 = 3 * GDN_WIDTH
    o2 = o1 + GDN_WIDTH
    o3 = o2 + GDN_HEADS
    o4 = o3 + GDN_HEADS
    o5 = o4 + 3 * FOX_WIDTH
    nb = T // FOX_BLK
    place_q = _placement(FOX_DIM, FOX_DIM + 1, 1.0)
    place_k = _placement(FOX_DIM + 2, FOX_DIM + 3, -1.0)
    const_q = _lane_const([FOX_DIM + 2, FOX_DIM + 3])
    const_k = _lane_const([FOX_DIM, FOX_DIM + 1])

    for l in range(depth):
        wi = w_in[l]
        w_gdn = jnp.concatenate([wi[:, :o2], jnp.repeat(wi[:, o2:o3], GDN_DIM, axis=1),
                                 jnp.repeat(wi[:, o3:o4], GDN_DIM, axis=1)], axis=1).astype(BF16)
        aneg = _lane_rep(-jnp.exp(a_log[l].astype(F32)), GDN_DIM)
        dtb = _lane_rep(dt_bias[l], GDN_DIM)
        qkv, gate, la, beta = _gdn_in(x, mix_norm[l][None, :], w_gdn, aneg, dtb)
        o_gdn = _gdn(qkv, la, beta, gate, conv_w[l].reshape(CONV_K, 3 * GDN_WIDTH),
                     gdn_out_norm[l])
        fq_w = _fox_layout(wi[:, o4:o4 + FOX_WIDTH])
        fk_w = _fox_layout(wi[:, o4 + FOX_WIDTH:o4 + 2 * FOX_WIDTH])
        fv_w = wi[:, o4 + 2 * FOX_WIDTH:o5].astype(BF16)
        ff_w = jnp.zeros((D, LANES), F32).at[:, :FOX_HEADS].set(wi[:, o5:]).astype(BF16)
        q_aug, k_aug, v_fox, cb = _fox_in(
            x, mix_norm[l][None, :], fq_w, fk_w, fv_w, ff_w, _pad_lanes(fox_f_bias[l]),
            _pad_lanes(fox_q_norm[l], FOX_DIM ** -0.5 * LOG2E), _pad_lanes(fox_k_norm[l]),
            place_q, place_k, const_q, const_k)
        cb_heads = jnp.broadcast_to(
            jnp.transpose(cb[:, :, 0, :FOX_HEADS], (0, 2, 1))[..., None], (B, FOX_HEADS, nb, LANES))
        o_fox = _fox_attn(q_aug, k_aug, v_fox, cb_heads)
        k_mem, v_mem = _mem_kv(mem, mem_norm[l][None, :], w_mkv[l].astype(BF16), mk_norm[l][None, :])
        x = _xattn(x, o_gdn, o_fox, w_out[l].astype(BF16), xattn_norm[l][None, :], w_mq[l].astype(BF16),
                   mq_norm[l][None, :], k_mem, v_mem, w_mo[l].astype(BF16))
        x = _peer(x.reshape(B * T, D), ffn_norm[l][None, :], w_pq[l].T.astype(BF16),
                  sub_keys[l].astype(BF16), expert_u[l].astype(BF16),
                  expert_v[l].T.astype(BF16)).reshape(B, T, D)
    return x
```

```python
import functools

import jax
import jax.numpy as jnp
from jax import lax
from jax.experimental import pallas as pl
from jax.experimental.pallas import tpu as pltpu

F32 = jnp.float32
BF16 = jnp.bfloat16
EPS = 1e-6
NEG_INF = float("-inf")
LOG2E = 1.4426950408889634

D_MODEL = 1024
CHUNK = 64
GDN_HEADS = 4
GDN_DIM = 128
GDN_WIDTH = GDN_HEADS * GDN_DIM
CONV_K = 4
FOX_HEADS = 8
FOX_DIM = 64
FOX_WIDTH = FOX_HEADS * FOX_DIM
MEM_HEADS = 4
MEM_DIM = 256
PEER_HEADS = 8
PEER_KEYS = 128
PEER_HALF = 128
PEER_TOPK = 16
PEER_EXPERTS = PEER_KEYS * PEER_KEYS
LANES = 128
HALO = 8
GDN_NCH = 4

TOK_TILE = 512
FOX_BLK = 512
PEER_TOK = 512
PEER_ECHUNK = 2048
PEER_SUB = 512
VMEM_LIMIT = 56 * 1024 * 1024


def _dot(a, b):
    return jnp.dot(a.astype(BF16), b.astype(BF16), preferred_element_type=F32)


def _dot_nt(a, b):
    return lax.dot_general(a.astype(BF16), b.astype(BF16), (((1,), (1,)), ((), ())),
                           preferred_element_type=F32)


def _split2(x):
    hi = x.astype(BF16)
    lo = (x - hi.astype(F32)).astype(BF16)
    return hi, lo


def _split3(x):
    hi = x.astype(BF16)
    r = x - hi.astype(F32)
    mid = r.astype(BF16)
    lo = (r - mid.astype(F32)).astype(BF16)
    return hi, mid, lo


def _dot_sel_left(sel, x):
    hi, mid, lo = _split3(x)
    d = functools.partial(jnp.dot, preferred_element_type=F32)
    return d(sel, hi) + d(sel, mid) + d(sel, lo)


def _dot_sel_right(x, sel, pieces=2):
    d = functools.partial(jnp.dot, preferred_element_type=F32)
    if pieces == 2:
        hi, lo = _split2(x)
        return d(hi, sel) + d(lo, sel)
    hi, mid, lo = _split3(x)
    return d(hi, sel) + d(mid, sel) + d(lo, sel)


def _sigmoid(x):
    return 1.0 / (1.0 + jnp.exp(-x))


def _silu(x):
    return x * _sigmoid(x)


def _softplus(x):
    return jnp.maximum(x, 0.0) + jnp.log(1.0 + jnp.exp(-jnp.abs(x)))


def _gelu_tanh(x):
    c = 0.7978845608028654
    k0 = -2.0 * c * LOG2E
    k1 = k0 * 0.044715
    return x / (1.0 + jnp.exp2(x * (k0 + k1 * (x * x))))


def _rms_rows(x, gain):
    ms = jnp.mean(x * x, axis=-1, keepdims=True)
    return x * lax.rsqrt(ms + EPS) * gain


def _iota2(shape, dim):
    return lax.broadcasted_iota(jnp.int32, shape, dim)


def _params(sem):
    return pltpu.CompilerParams(dimension_semantics=sem, vmem_limit_bytes=VMEM_LIMIT)


def _gdn_in_kernel(x_ref, g_ref, w_ref, aneg_ref, dtb_ref, qkv_ref, gate_ref, la_ref, beta_ref):
    h = _rms_rows(x_ref[0], g_ref[...]).astype(BF16)
    o_gate = 3 * GDN_WIDTH
    o_a = o_gate + GDN_WIDTH
    o_b = o_a + GDN_WIDTH
    qkv_ref[0] = jnp.dot(h, w_ref[:, :o_gate], preferred_element_type=F32)
    gate_ref[0] = _silu(jnp.dot(h, w_ref[:, o_gate:o_a], preferred_element_type=F32))
    a = jnp.dot(h, w_ref[:, o_a:o_b], preferred_element_type=F32)
    la_ref[0] = aneg_ref[...] * _softplus(a + dtb_ref[...])
    b = jnp.dot(h, w_ref[:, o_b:], preferred_element_type=F32)
    beta_ref[0] = _sigmoid(b)


def _gdn_in(x, gain, w, aneg, dtb):
    B, T, D = x.shape
    n = w.shape[1]
    tm = TOK_TILE
    row = lambda b, i: (b, i, 0)
    fixed = lambda b, i: (0, 0)
    return pl.pallas_call(
        _gdn_in_kernel,
        grid=(B, T // tm),
        in_specs=[pl.BlockSpec((1, tm, D), row), pl.BlockSpec((1, D), fixed),
                  pl.BlockSpec((D, n), fixed), pl.BlockSpec((1, GDN_WIDTH), fixed),
                  pl.BlockSpec((1, GDN_WIDTH), fixed)],
        out_specs=[pl.BlockSpec((1, tm, 3 * GDN_WIDTH), row), pl.BlockSpec((1, tm, GDN_WIDTH), row),
                   pl.BlockSpec((1, tm, GDN_WIDTH), row), pl.BlockSpec((1, tm, GDN_WIDTH), row)],
        out_shape=[jax.ShapeDtypeStruct((B, T, 3 * GDN_WIDTH), F32),
                   jax.ShapeDtypeStruct((B, T, GDN_WIDTH), F32),
                   jax.ShapeDtypeStruct((B, T, GDN_WIDTH), F32),
                   jax.ShapeDtypeStruct((B, T, GDN_WIDTH), F32)],
        compiler_params=_params(("parallel", "parallel")),
        name="gdn_in",
    )(x, gain, w, aneg, dtb)


def _bdot(a, b):
    return jnp.einsum("uik,ukj->uij", a.astype(BF16), b.astype(BF16), preferred_element_type=F32)


def _bdot_nt(a, b):
    return jnp.einsum("uik,ujk->uij", a.astype(BF16), b.astype(BF16), preferred_element_type=F32)


def _gdn_kernel(qkv_ref, la_ref, beta_ref, gate_ref, cw_ref, gn_ref, bd_ref, btri_ref, out_ref, xbuf, state):
    C = CHUNK
    N = GDN_NCH
    H = GDN_HEADS
    R = N * C

    @pl.when(pl.program_id(1) == 0)
    def _():
        xbuf[0:HALO, :] = jnp.zeros((HALO, 3 * GDN_WIDTH), F32)
        state[...] = jnp.zeros(state.shape, F32)

    xbuf[HALO:HALO + R, :] = qkv_ref[0]
    conv = jnp.zeros((R, 3 * GDN_WIDTH), F32)
    for j in range(CONV_K):
        off = HALO - (CONV_K - 1) + j
        conv = conv + cw_ref[j:j + 1, :] * xbuf[off:off + R, :]
    xbuf[0:HALO, :] = xbuf[R:R + HALO, :]
    y = _silu(conv)

    bd = bd_ref[...]
    q_all = y[:, :GDN_WIDTH]
    k_all = y[:, GDN_WIDTH:2 * GDN_WIDTH]
    qs_all = q_all * lax.rsqrt(_dot_sel_right(q_all * q_all, bd) + EPS) * (GDN_DIM ** -0.5)
    kn_all = k_all * lax.rsqrt(_dot_sel_right(k_all * k_all, bd) + EPS)
    g_all = _dot_sel_left(btri_ref[...], la_ref[0])

    def units(arr, col0=0):
        return jnp.stack([arr[c * C:(c + 1) * C, col0 + h * GDN_DIM:col0 + (h + 1) * GDN_DIM]
                          for c in range(N) for h in range(H)])

    qs = units(qs_all)
    kn = units(kn_all)
    v = units(y, 2 * GDN_WIDTH)
    bt = units(beta_ref[0])
    g = units(g_all)

    row = _iota2((C, C), 0)
    col = _iota2((C, C), 1)
    causal = (col <= row)[None]
    strict = (col < row)[None]
    eye = jnp.where(row == col, 1.0, 0.0)[None]
    g_col = g[:, :, :C]
    diff = g_col - jnp.swapaxes(g_col, 1, 2)
    decay = jnp.where(causal, jnp.exp(jnp.where(causal, diff, 0.0)), 0.0)
    eg = jnp.exp(g)
    kb = kn * bt
    a = jnp.where(strict, _bdot_nt(kb, kn) * decay, 0.0)
    t_inv = None
    for lvl in range(1, C.bit_length()):
        same_blk = (row >> lvl) == (col >> lvl)
        lower_left = ((row >> (lvl - 1)) & 1) > ((col >> (lvl - 1)) & 1)
        a_n = jnp.where((same_blk & lower_left)[None], a, 0.0)
        t_inv = eye - a_n if t_inv is None else t_inv - _bdot(_bdot(t_inv, a_n), t_inv)
    w = _bdot(t_inv, kb * eg)
    u_all = _bdot(t_inv, v * bt)
    qk = jnp.where(causal, _bdot_nt(qs, kn) * decay, 0.0)
    qd = qs * eg
    g_last = g[:, C - 1:C, :]
    kd_t = jnp.swapaxes(kn * jnp.exp(g_last - g), 1, 2)
    eg_last = jnp.exp(g_last)

    s = [state[h] for h in range(H)]
    o_rows = []
    for c in range(N):
        o_heads = []
        for h in range(H):
            u = c * H + h
            v_new = u_all[u] - _dot(w[u], s[h])
            o_heads.append(_dot(qd[u], s[h]) + _dot(qk[u], v_new))
            s[h] = s[h] * eg_last[u] + _dot(kd_t[u], v_new)
        o_rows.append(jnp.concatenate(o_heads, axis=1))
    for h in range(H):
        state[h] = s[h]
    o = jnp.concatenate(o_rows, axis=0)
    ms = _dot_sel_right(o * o, bd) * (1.0 / GDN_DIM)
    out_ref[0] = o * lax.rsqrt(ms + EPS) * gn_ref[...] * gate_ref[0]


def _gdn(qkv, la, beta, gate, conv_w, gnorm):
    B, T, _ = qkv.shape
    R = GDN_NCH * CHUNK
    row = lambda b, c: (b, c, 0)
    fixed = lambda b, c: (0, 0)
    idx = jnp.arange(GDN_WIDTH)
    bd = (idx[:, None] // GDN_DIM == idx[None, :] // GDN_DIM).astype(BF16)
    r = jnp.arange(R)
    btri = ((r[:, None] // CHUNK == r[None, :] // CHUNK) & (r[None, :] <= r[:, None])).astype(BF16)
    return pl.pallas_call(
        _gdn_kernel,
        grid=(B, T // R),
        in_specs=[pl.BlockSpec((1, R, 3 * GDN_WIDTH), row), pl.BlockSpec((1, R, GDN_WIDTH), row),
                  pl.BlockSpec((1, R, GDN_WIDTH), row), pl.BlockSpec((1, R, GDN_WIDTH), row),
                  pl.BlockSpec((CONV_K, 3 * GDN_WIDTH), fixed), pl.BlockSpec((1, GDN_WIDTH), fixed),
                  pl.BlockSpec((GDN_WIDTH, GDN_WIDTH), fixed), pl.BlockSpec((R, R), fixed)],
        out_specs=pl.BlockSpec((1, R, GDN_WIDTH), row),
        out_shape=jax.ShapeDtypeStruct((B, T, GDN_WIDTH), F32),
        scratch_shapes=[pltpu.VMEM((HALO + R, 3 * GDN_WIDTH), F32),
                        pltpu.VMEM((GDN_HEADS, GDN_DIM, GDN_DIM), F32)],
        compiler_params=_params(("parallel", "arbitrary")),
        name="gdn",
    )(qkv, la, beta, gate, conv_w, jnp.tile(gnorm.astype(F32), GDN_HEADS)[None, :], bd, btri)


def _fox_in_kernel(x_ref, g_ref, wq_ref, wk_ref, wv_ref, wf_ref, fb_ref, gq_ref, gk_ref, pq_ref, pk_ref,
                   cq_ref, ck_ref, q_ref, k_ref, v_ref, cb_ref, carry):
    tm = FOX_BLK
    i = pl.program_id(1)

    @pl.when(i == 0)
    def _():
        carry[...] = jnp.zeros(carry.shape, F32)

    h = _rms_rows(x_ref[0], g_ref[...]).astype(BF16)
    v_ref[0] = jnp.dot(h, wv_ref[...], preferred_element_type=F32).astype(BF16)
    zf = jnp.dot(h, wf_ref[...], preferred_element_type=F32) + fb_ref[...]
    logf = -_softplus(-zf)
    row = _iota2((tm, tm), 0)
    col = _iota2((tm, tm), 1)
    tri_incl = jnp.where(col <= row, 1.0, 0.0).astype(BF16)
    c_loc = _dot_sel_left(tri_incl, logf * LOG2E)
    cb_ref[0, 0] = jnp.broadcast_to(carry[0:1, :], (8, LANES))
    carry[...] = carry[...] + jnp.broadcast_to(c_loc[tm - 1:tm, :], carry.shape)
    hi, lo = _split2(c_loc)
    hl = jnp.concatenate([hi, lo], axis=1)
    mean_sel = jnp.full((LANES, LANES), 1.0 / FOX_DIM, BF16)
    for hd in range(FOX_HEADS):
        sl = slice(hd * LANES, (hd + 1) * LANES)
        for (w_r, gain_r, place_r, const_r, o_r) in ((wq_ref, gq_ref, pq_ref, cq_ref, q_ref),
                                                    (wk_ref, gk_ref, pk_ref, ck_ref, k_ref)):
            p = jnp.dot(h, w_r[:, sl], preferred_element_type=F32)
            ms = _dot_sel_right(p * p, mean_sel)
            pn = p * lax.rsqrt(ms + EPS) * gain_r[...]
            aug = pn + jnp.dot(hl, place_r[hd], preferred_element_type=F32) + const_r[...]
            o_r[0, hd] = aug.astype(BF16)


def _fox_in(x, gain, wq, wk, wv, wf, fb, gq, gk, pq, pk, cq, ck):
    B, T, D = x.shape
    tm = FOX_BLK
    nb = T // tm
    row = lambda b, i: (b, i, 0)
    fixed2 = lambda b, i: (0, 0)
    fixed3 = lambda b, i: (0, 0, 0)
    hrow = lambda b, i: (b, 0, i, 0)
    return pl.pallas_call(
        _fox_in_kernel,
        grid=(B, nb),
        in_specs=[pl.BlockSpec((1, tm, D), row), pl.BlockSpec((1, D), fixed2),
                  pl.BlockSpec((D, FOX_HEADS * LANES), fixed2), pl.BlockSpec((D, FOX_HEADS * LANES), fixed2),
                  pl.BlockSpec((D, FOX_WIDTH), fixed2), pl.BlockSpec((D, LANES), fixed2),
                  pl.BlockSpec((1, LANES), fixed2), pl.BlockSpec((1, LANES), fixed2),
                  pl.BlockSpec((1, LANES), fixed2),
                  pl.BlockSpec((FOX_HEADS, 2 * LANES, LANES), fixed3),
                  pl.BlockSpec((FOX_HEADS, 2 * LANES, LANES), fixed3),
                  pl.BlockSpec((1, LANES), fixed2), pl.BlockSpec((1, LANES), fixed2)],
        out_specs=[pl.BlockSpec((1, FOX_HEADS, tm, LANES), hrow), pl.BlockSpec((1, FOX_HEADS, tm, LANES), hrow),
                   pl.BlockSpec((1, tm, FOX_WIDTH), row),
                   pl.BlockSpec((1, 1, 8, LANES), lambda b, i: (b, i, 0, 0))],
        out_shape=[jax.ShapeDtypeStruct((B, FOX_HEADS, T, LANES), BF16),
                   jax.ShapeDtypeStruct((B, FOX_HEADS, T, LANES), BF16),
                   jax.ShapeDtypeStruct((B, T, FOX_WIDTH), BF16),
                   jax.ShapeDtypeStruct((B, nb, 8, LANES), F32)],
        scratch_shapes=[pltpu.VMEM((8, LANES), F32)],
        compiler_params=_params(("parallel", "arbitrary")),
        name="fox_in",
    )(x, gain, wq, wk, wv, wf, fb, gq, gk, pq, pk, cq, ck)


def _fox_attn_kernel(q_ref, k_ref, v_ref, cb_ref, o_ref, m_s, l_s, acc_s):
    blk = FOX_BLK
    reps = blk // LANES
    i = pl.program_id(2)
    m_s[...] = jnp.full(m_s.shape, NEG_INF, F32)
    l_s[...] = jnp.zeros(l_s.shape, F32)
    acc_s[...] = jnp.zeros(acc_s.shape, F32)

    def step(j, masked):
        start = pl.multiple_of(j * blk, blk)
        vj = v_ref[0, pl.ds(start, blk), :]
        for hh in range(2):
            kj = k_ref[0, hh, pl.ds(start, blk), :]
            s = _dot_nt(q_ref[0, hh], kj)
            if masked:
                s = jnp.where(_iota2((blk, blk), 1) <= _iota2((blk, blk), 0), s, NEG_INF)
            delta = cb_ref[0, hh, pl.ds(i, 1), :] - cb_ref[0, hh, pl.ds(j, 1), :]
            m_old = m_s[hh]
            m_new = jnp.maximum(m_old, jnp.max(s, axis=-1, keepdims=True) + delta)
            shift = delta - m_new
            p = jnp.exp2(s + jnp.concatenate([shift] * reps, axis=1))
            alpha = jnp.exp2(m_old - m_new)
            l_s[hh] = alpha * l_s[hh] + jnp.sum(p, axis=-1, keepdims=True)
            acc_s[hh] = alpha * acc_s[hh] + jnp.dot(p.astype(BF16), vj, preferred_element_type=F32)
            m_s[hh] = m_new

    def step2(j):
        start0 = pl.multiple_of(j * blk, blk)
        start1 = pl.multiple_of((j + 1) * blk, blk)
        v0 = v_ref[0, pl.ds(start0, blk), :]
        v1 = v_ref[0, pl.ds(start1, blk), :]
        for hh in range(2):
            q = q_ref[0, hh]
            s0 = _dot_nt(q, k_ref[0, hh, pl.ds(start0, blk), :])
            s1 = _dot_nt(q, k_ref[0, hh, pl.ds(start1, blk), :])
            cbi = cb_ref[0, hh, pl.ds(i, 1), :]
            d0 = cbi - cb_ref[0, hh, pl.ds(j, 1), :]
            d1 = cbi - cb_ref[0, hh, pl.ds(j + 1, 1), :]
            m_old = m_s[hh]
            m_new = jnp.maximum(m_old, jnp.maximum(jnp.max(s0, axis=-1, keepdims=True) + d0,
                                                   jnp.max(s1, axis=-1, keepdims=True) + d1))
            p0 = jnp.exp2(s0 + jnp.concatenate([d0 - m_new] * reps, axis=1))
            p1 = jnp.exp2(s1 + jnp.concatenate([d1 - m_new] * reps, axis=1))
            alpha = jnp.exp2(m_old - m_new)
            l_s[hh] = alpha * l_s[hh] + (jnp.sum(p0, axis=-1, keepdims=True) + jnp.sum(p1, axis=-1, keepdims=True))
            pv = (jnp.dot(p0.astype(BF16), v0, preferred_element_type=F32)
                  + jnp.dot(p1.astype(BF16), v1, preferred_element_type=F32))
            acc_s[hh] = alpha * acc_s[hh] + pv
            m_s[hh] = m_new

    def body(k, carry):
        step2(2 * k)
        return carry

    lax.fori_loop(0, i // 2, body, 0)

    @pl.when(i % 2 == 1)
    def _():
        step(i - 1, False)

    step(i, True)
    lane = _iota2((blk, LANES), 1)
    o_ref[0] = jnp.where(lane < FOX_DIM, acc_s[0] / l_s[0], acc_s[1] / l_s[1])


def _fox_attn(q_aug, k_aug, v, cb):
    B, H, T, _ = q_aug.shape
    blk = FOX_BLK
    nb = T // blk
    return pl.pallas_call(
        _fox_attn_kernel,
        grid=(B, H // 2, nb),
        in_specs=[pl.BlockSpec((1, 2, blk, LANES), lambda b, p, i: (b, p, i, 0)),
                  pl.BlockSpec((1, 2, T, LANES), lambda b, p, i: (b, p, 0, 0)),
                  pl.BlockSpec((1, T, LANES), lambda b, p, i: (b, 0, p)),
                  pl.BlockSpec((1, 2, nb, LANES), lambda b, p, i: (b, p, 0, 0))],
        out_specs=pl.BlockSpec((1, blk, LANES), lambda b, p, i: (b, i, p)),
        out_shape=jax.ShapeDtypeStruct((B, T, FOX_WIDTH), F32),
        scratch_shapes=[pltpu.VMEM((2, blk, LANES), F32), pltpu.VMEM((2, blk, LANES), F32),
                        pltpu.VMEM((2, blk, LANES), F32)],
        compiler_params=_params(("parallel", "parallel", "arbitrary")),
        name="fox_attn",
    )(q_aug, k_aug, v, cb)


def _mem_kv_kernel(mem_ref, g_ref, w_ref, gk_ref, k_ref, v_ref):
    h = _rms_rows(mem_ref[0], g_ref[...]).astype(BF16)
    kv = jnp.dot(h, w_ref[...], preferred_element_type=F32)
    for hd in range(MEM_HEADS):
        kh = kv[:, hd * MEM_DIM:(hd + 1) * MEM_DIM]
        k_ref[0, hd] = _rms_rows(kh, gk_ref[...]).astype(BF16)
        v_ref[0, hd] = kv[:, D_MODEL + hd * MEM_DIM:D_MODEL + (hd + 1) * MEM_DIM].astype(BF16)


def _mem_kv(mem, gain, w, gk):
    B, M, D = mem.shape
    fixed = lambda b: (0, 0)
    return pl.pallas_call(
        _mem_kv_kernel,
        grid=(B,),
        in_specs=[pl.BlockSpec((1, M, D), lambda b: (b, 0, 0)), pl.BlockSpec((1, D), fixed),
                  pl.BlockSpec((D, 2 * D), fixed), pl.BlockSpec((1, MEM_DIM), fixed)],
        out_specs=[pl.BlockSpec((1, MEM_HEADS, M, MEM_DIM), lambda b: (b, 0, 0, 0)),
                   pl.BlockSpec((1, MEM_HEADS, M, MEM_DIM), lambda b: (b, 0, 0, 0))],
        out_shape=[jax.ShapeDtypeStruct((B, MEM_HEADS, M, MEM_DIM), BF16),
                   jax.ShapeDtypeStruct((B, MEM_HEADS, M, MEM_DIM), BF16)],
        compiler_params=_params(("parallel",)),
        name="mem_kv",
    )(mem, gain, w, gk)


def _xattn_kernel(x_ref, og_ref, of_ref, wo_ref, g_ref, wq_ref, gq_ref, k_ref, v_ref, wmo_ref, out_ref):
    x1 = (x_ref[0] + _dot(og_ref[0], wo_ref[:GDN_WIDTH, :]) + _dot(of_ref[0], wo_ref[GDN_WIDTH:, :]))
    h = _rms_rows(x1, g_ref[...]).astype(BF16)
    q = jnp.dot(h, wq_ref[...], preferred_element_type=F32)
    heads = []
    for hd in range(MEM_HEADS):
        sl = slice(hd * MEM_DIM, (hd + 1) * MEM_DIM)
        qn = _rms_rows(q[:, sl], gq_ref[...]) * (MEM_DIM ** -0.5)
        s = _dot_nt(qn, k_ref[0, hd])
        m = jnp.max(s, axis=-1, keepdims=True)
        p = jnp.exp(s - m)
        p = p / jnp.sum(p, axis=-1, keepdims=True)
        heads.append(_dot(p, v_ref[0, hd]).astype(BF16))
    o = jnp.concatenate(heads, axis=1)
    out_ref[0] = x1 + jnp.dot(o, wmo_ref[...], preferred_element_type=F32)


def _xattn(x, og, of, wo, gain, wq, gq, k, v, wmo):
    B, T, D = x.shape
    tm = TOK_TILE
    M = k.shape[2]
    row = lambda b, i: (b, i, 0)
    fixed = lambda b, i: (0, 0)
    kvs = lambda b, i: (b, 0, 0, 0)
    return pl.pallas_call(
        _xattn_kernel,
        grid=(B, T // tm),
        in_specs=[pl.BlockSpec((1, tm, D), row), pl.BlockSpec((1, tm, GDN_WIDTH), row),
                  pl.BlockSpec((1, tm, FOX_WIDTH), row), pl.BlockSpec((D, D), fixed),
                  pl.BlockSpec((1, D), fixed), pl.BlockSpec((D, D), fixed), pl.BlockSpec((1, MEM_DIM), fixed),
                  pl.BlockSpec((1, MEM_HEADS, M, MEM_DIM), kvs), pl.BlockSpec((1, MEM_HEADS, M, MEM_DIM), kvs),
                  pl.BlockSpec((D, D), fixed)],
        out_specs=pl.BlockSpec((1, tm, D), row),
        out_shape=jax.ShapeDtypeStruct((B, T, D), F32),
        compiler_params=_params(("parallel", "parallel")),
        name="xattn",
    )(x, og, of, wo, gain, wq, gq, k, v, wmo)


def _top_values(s, n, rows_ref, want_rank=False):
    m = jnp.max(s, axis=0, keepdims=True)
    rows_ref[0:1, :] = m
    rank = None
    for r in range(1, n):
        below = s < m
        if want_rank:
            rank = jnp.where(below, 1.0, 0.0) if rank is None else rank + jnp.where(below, 1.0, 0.0)
        m = jnp.max(jnp.where(below, s, NEG_INF), axis=0, keepdims=True)
        rows_ref[r:r + 1, :] = m
    if want_rank:
        rank = rank + jnp.where(s < m, 1.0, 0.0)
    return rank


PEER_NSEL = PEER_TOPK + 1
PEER_PAIRS = [(i, j) for i in range(PEER_NSEL) for j in range(PEER_NSEL) if (i + 1) * (j + 1) <= PEER_NSEL]
PEER_CAND_ROWS = -(-len(PEER_PAIRS) // 8) * 8
PEER_TOP_ROWS = -(-PEER_NSEL // 8) * 8


def _peer_kernel(x_ref, g_ref, wqt_ref, keys_ref, u_ref, vt_ref, out_ref,
                 h_s, n_s, e1_s, e2_s, rank_s, top_s, cand_s, acc_s, sc_s, w_s):
    tm = PEER_TOK
    j = pl.program_id(1)
    nj = pl.num_programs(1)

    @pl.when(j == 0)
    def _():
        h = _rms_rows(x_ref[...], g_ref[...]).astype(BF16)
        h_s[...] = h
        acc_s[...] = jnp.zeros(acc_s.shape, F32)
        cand_s[...] = jnp.full(cand_s.shape, NEG_INF, F32)
        for hd in range(PEER_HEADS):
            scores = []
            for half in range(2):
                r0 = (hd * 2 + half) * PEER_HALF
                qt = _dot_nt(wqt_ref[r0:r0 + PEER_HALF, :], h)
                scores.append(_dot(keys_ref[half, hd], qt))
            s1, s2 = scores
            _top_values(s1, PEER_NSEL, top_s.at[0])
            rank_s[hd] = _top_values(s2, PEER_NSEL, top_s.at[1], want_rank=True).astype(BF16)
            for r, (i1, i2) in enumerate(PEER_PAIRS):
                cand_s[r:r + 1, :] = top_s[0, i1:i1 + 1, :] + top_s[1, i2:i2 + 1, :]
            cand = cand_s[...]
            _top_values(cand, PEER_NSEL, top_s.at[2])
            tau = 0.5 * (top_s[2, PEER_TOPK - 1:PEER_TOPK, :] + top_s[2, PEER_TOPK:PEER_TOPK + 1, :])
            v1max = top_s[0, 0:1, :]
            v2max = top_s[1, 0:1, :]
            z = jnp.sum(jnp.where(cand >= tau, jnp.exp(cand - (v1max + v2max)), 0.0), axis=0, keepdims=True)
            e1_s[hd] = jnp.exp(s1 - v1max)
            e2_s[hd] = (jnp.exp(s2 - v2max) / z).astype(BF16)
            n = None
            for r in range(PEER_TOPK):
                hit = jnp.where(s1 >= tau - top_s[1, r:r + 1, :], 1.0, 0.0)
                n = hit if n is None else n + hit
            n_s[hd] = n

    hb = h_s[...]
    n_sub = PEER_ECHUNK // PEER_SUB
    a_per_sub = PEER_SUB // PEER_KEYS
    rep = PEER_KEYS // 16

    def score(sub):
        sc_s[sub % 2] = _dot_nt(u_ref[sub * PEER_SUB:(sub + 1) * PEER_SUB, :], hb)

    score(0)
    for sub in range(n_sub):
        slot = sub % 2
        if sub + 1 < n_sub:
            score(sub + 1)
        for ai in range(a_per_sub):
            a_idx = j * (PEER_ECHUNK // PEER_KEYS) + sub * a_per_sub + ai
            ar = slice(ai * PEER_KEYS, (ai + 1) * PEER_KEYS)
            gate = None
            for hd in range(PEER_HEADS):
                n16 = jnp.broadcast_to(n_s[hd, pl.ds(a_idx, 1), :], (16, tm)).astype(BF16)
                e16 = jnp.broadcast_to(e1_s[hd, pl.ds(a_idx, 1), :], (16, tm)).astype(BF16)
                n_a = jnp.concatenate([n16] * rep, axis=0)
                e1_a = jnp.concatenate([e16] * rep, axis=0)
                term = jnp.where(rank_s[hd] < n_a, e1_a * e2_s[hd], jnp.zeros((), BF16))
                gate = term if gate is None else gate + term
            w_s[slot, ar, :] = _gelu_tanh(sc_s[slot, ar, :]).astype(BF16) * gate
        rows = slice(sub * PEER_SUB, (sub + 1) * PEER_SUB)
        acc_s[...] += jnp.dot(vt_ref[:, rows], w_s[slot], preferred_element_type=F32)

    @pl.when(j == nj - 1)
    def _():
        out_ref[...] = x_ref[...] + acc_s[...].T


def _peer(x2d, gain, wqt, keys, u, vt):
    M, D = x2d.shape
    tm = PEER_TOK
    E = u.shape[0]
    ec = PEER_ECHUNK
    return pl.pallas_call(
        _peer_kernel,
        grid=(M // tm, E // ec),
        in_specs=[pl.BlockSpec((tm, D), lambda i, j: (i, 0)), pl.BlockSpec((1, D), lambda i, j: (0, 0)),
                  pl.BlockSpec((PEER_HEADS * 2 * PEER_HALF, D), lambda i, j: (0, 0)),
                  pl.BlockSpec((2, PEER_HEADS, PEER_KEYS, PEER_HALF), lambda i, j: (0, 0, 0, 0)),
                  pl.BlockSpec((ec, D), lambda i, j: (j, 0)), pl.BlockSpec((D, ec), lambda i, j: (0, j))],
        out_specs=pl.BlockSpec((tm, D), lambda i, j: (i, 0)),
        out_shape=jax.ShapeDtypeStruct((M, D), F32),
        scratch_shapes=[pltpu.VMEM((tm, D), BF16),
                        pltpu.VMEM((PEER_HEADS, PEER_KEYS, tm), F32),
                        pltpu.VMEM((PEER_HEADS, PEER_KEYS, tm), F32),
                        pltpu.VMEM((PEER_HEADS, PEER_KEYS, tm), BF16),
                        pltpu.VMEM((PEER_HEADS, PEER_KEYS, tm), BF16),
                        pltpu.VMEM((3, PEER_TOP_ROWS, tm), F32),
                        pltpu.VMEM((PEER_CAND_ROWS, tm), F32),
                        pltpu.VMEM((D, tm), F32),
                        pltpu.VMEM((2, PEER_SUB, tm), F32),
                        pltpu.VMEM((2, PEER_SUB, tm), BF16)],
        compiler_params=_params(("parallel", "arbitrary")),
        name="peer",
    )(x2d, gain, wqt, keys, u, vt)


def _lane_rep(v, width):
    return jnp.repeat(v.astype(F32), width)[None, :]


def _fox_layout(w_heads):
    D = w_heads.shape[0]
    out = jnp.zeros((D, FOX_HEADS, LANES), F32)
    out = out.at[:, :, :FOX_DIM].set(w_heads.reshape(D, FOX_HEADS, FOX_DIM))
    return out.reshape(D, FOX_HEADS * LANES).astype(BF16)


def _placement(lane_hi, lane_lo, sign):
    p = jnp.zeros((FOX_HEADS, 2 * LANES, LANES), F32)
    hs = jnp.arange(FOX_HEADS)
    p = p.at[hs, hs, lane_hi].set(sign)
    p = p.at[hs, LANES + hs, lane_lo].set(sign)
    return p.astype(BF16)


def _lane_const(lanes):
    c = jnp.zeros((1, LANES), F32)
    return c.at[0, jnp.array(lanes)].set(1.0)


def _pad_lanes(v, scale=1.0):
    return jnp.zeros((1, LANES), F32).at[0, :v.shape[0]].set(v.astype(F32) * scale)


def kernel(x, mem, mix_norm, w_in, conv_w, a_log, dt_bias, gdn_out_norm, fox_q_norm, fox_k_norm, fox_f_bias,
           w_out, xattn_norm, mem_norm, w_mq, w_mkv, mq_norm, mk_norm, w_mo, ffn_norm, w_pq, sub_keys,
           expert_u, expert_v):
    B, T, D = x.shape
    depth = w_in.shape[0]
    o1 = 3 * GDN_WIDTH
    o2 = o1 + GDN_WIDTH
    o3 = o2 + GDN_HEADS
    o4 = o3 + GDN_HEADS
    o5 = o4 + 3 * FOX_WIDTH
    nb = T // FOX_BLK
    place_q = _placement(FOX_DIM, FOX_DIM + 1, 1.0)
    place_k = _placement(FOX_DIM + 2, FOX_DIM + 3, -1.0)
    const_q = _lane_const([FOX_DIM + 2, FOX_DIM + 3])
    const_k = _lane_const([FOX_DIM, FOX_DIM + 1])

    for l in range(depth):
        wi = w_in[l]
        w_gdn = jnp.concatenate([wi[:, :o2], jnp.repeat(wi[:, o2:o3], GDN_DIM, axis=1),
                                 jnp.repeat(wi[:, o3:o4], GDN_DIM, axis=1)], axis=1).astype(BF16)
        aneg = _lane_rep(-jnp.exp(a_log[l].astype(F32)), GDN_DIM)
        dtb = _lane_rep(dt_bias[l], GDN_DIM)
        qkv, gate, la, beta = _gdn_in(x, mix_norm[l][None, :], w_gdn, aneg, dtb)
        o_gdn = _gdn(qkv, la, beta, gate, conv_w[l].reshape(CONV_K, 3 * GDN_WIDTH),
                     gdn_out_norm[l])
        fq_w = _fox_layout(wi[:, o4:o4 + FOX_WIDTH])
        fk_w = _fox_layout(wi[:, o4 + FOX_WIDTH:o4 + 2 * FOX_WIDTH])
        fv_w = wi[:, o4 + 2 * FOX_WIDTH:o5].astype(BF16)
        ff_w = jnp.zeros((D, LANES), F32).at[:, :FOX_HEADS].set(wi[:, o5:]).astype(BF16)
        q_aug, k_aug, v_fox, cb = _fox_in(
            x, mix_norm[l][None, :], fq_w, fk_w, fv_w, ff_w, _pad_lanes(fox_f_bias[l]),
            _pad_lanes(fox_q_norm[l], FOX_DIM ** -0.5 * LOG2E), _pad_lanes(fox_k_norm[l]),
            place_q, place_k, const_q, const_k)
        cb_heads = jnp.broadcast_to(
            jnp.transpose(cb[:, :, 0, :FOX_HEADS], (0, 2, 1))[..., None], (B, FOX_HEADS, nb, LANES))
        o_fox = _fox_attn(q_aug, k_aug, v_fox, cb_heads)
        k_mem, v_mem = _mem_kv(mem, mem_norm[l][None, :], w_mkv[l].astype(BF16), mk_norm[l][None, :])
        x = _xattn(x, o_gdn, o_fox, w_out[l].astype(BF16), xattn_norm[l][None, :], w_mq[l].astype(BF16),
                   mq_norm[l][None, :], k_mem, v_mem, w_mo[l].astype(BF16))
        x = _peer(x.reshape(B * T, D), ffn_norm[l][None, :], w_pq[l].T.astype(BF16),
                  sub_keys[l].astype(BF16), expert_u[l].astype(BF16),
                  expert_v[l].T.astype(BF16)).reshape(B, T, D)
    return x
```

```python
import functools

import jax
import jax.numpy as jnp
from jax import lax
from jax.experimental import pallas as pl
from jax.experimental.pallas import tpu as pltpu

F32 = jnp.float32
BF16 = jnp.bfloat16
EPS = 1e-6
NEG_INF = float("-inf")
LOG2E = 1.4426950408889634

D_MODEL = 1024
CHUNK = 64
GDN_HEADS = 4
GDN_DIM = 128
GDN_WIDTH = GDN_HEADS * GDN_DIM
CONV_K = 4
FOX_HEADS = 8
FOX_DIM = 64
FOX_WIDTH = FOX_HEADS * FOX_DIM
MEM_HEADS = 4
MEM_DIM = 256
PEER_HEADS = 8
PEER_KEYS = 128
PEER_HALF = 128
PEER_TOPK = 16
PEER_EXPERTS = PEER_KEYS * PEER_KEYS
LANES = 128
HALO = 8
GDN_NCH = 4

TOK_TILE = 512
FOX_BLK = 512
PEER_TOK = 512
PEER_ECHUNK = 2048
PEER_SUB = 512
VMEM_LIMIT = 56 * 1024 * 1024


def _dot(a, b):
    return jnp.dot(a.astype(BF16), b.astype(BF16), preferred_element_type=F32)


def _dot_nt(a, b):
    return lax.dot_general(a.astype(BF16), b.astype(BF16), (((1,), (1,)), ((), ())),
                           preferred_element_type=F32)


def _split2(x):
    hi = x.astype(BF16)
    lo = (x - hi.astype(F32)).astype(BF16)
    return hi, lo


def _split3(x):
    hi = x.astype(BF16)
    r = x - hi.astype(F32)
    mid = r.astype(BF16)
    lo = (r - mid.astype(F32)).astype(BF16)
    return hi, mid, lo


def _dot_sel_left(sel, x):
    hi, mid, lo = _split3(x)
    d = functools.partial(jnp.dot, preferred_element_type=F32)
    return d(sel, hi) + d(sel, mid) + d(sel, lo)


def _dot_sel_right(x, sel, pieces=2):
    d = functools.partial(jnp.dot, preferred_element_type=F32)
    if pieces == 2:
        hi, lo = _split2(x)
        return d(hi, sel) + d(lo, sel)
    hi, mid, lo = _split3(x)
    return d(hi, sel) + d(mid, sel) + d(lo, sel)


def _sigmoid(x):
    return 1.0 / (1.0 + jnp.exp(-x))


def _silu(x):
    return x * _sigmoid(x)


def _softplus(x):
    return jnp.maximum(x, 0.0) + jnp.log(1.0 + jnp.exp(-jnp.abs(x)))


def _gelu_tanh(x):
    c = 0.7978845608028654
    k0 = -2.0 * c * LOG2E
    k1 = k0 * 0.044715
    return x / (1.0 + jnp.exp2(x * (k0 + k1 * (x * x))))


def _rms_rows(x, gain):
    ms = jnp.mean(x * x, axis=-1, keepdims=True)
    return x * lax.rsqrt(ms + EPS) * gain


def _iota2(shape, dim):
    return lax.broadcasted_iota(jnp.int32, shape, dim)


def _params(sem):
    return pltpu.CompilerParams(dimension_semantics=sem, vmem_limit_bytes=VMEM_LIMIT)


def _gdn_in_kernel(x_ref, g_ref, w_ref, aneg_ref, dtb_ref, qkv_ref, gate_ref, la_ref, beta_ref):
    h = _rms_rows(x_ref[0], g_ref[...]).astype(BF16)
    o_gate = 3 * GDN_WIDTH
    o_a = o_gate + GDN_WIDTH
    o_b = o_a + GDN_WIDTH
    qkv_ref[0] = jnp.dot(h, w_ref[:, :o_gate], preferred_element_type=F32)
    gate_ref[0] = _silu(jnp.dot(h, w_ref[:, o_gate:o_a], preferred_element_type=F32))
    a = jnp.dot(h, w_ref[:, o_a:o_b], preferred_element_type=F32)
    la_ref[0] = aneg_ref[...] * _softplus(a + dtb_ref[...])
    b = jnp.dot(h, w_ref[:, o_b:], preferred_element_type=F32)
    beta_ref[0] = _sigmoid(b)


def _gdn_in(x, gain, w, aneg, dtb):
    B, T, D = x.shape
    n = w.shape[1]
    tm = TOK_TILE
    row = lambda b, i: (b, i, 0)
    fixed = lambda b, i: (0, 0)
    return pl.pallas_call(
        _gdn_in_kernel,
        grid=(B, T // tm),
        in_specs=[pl.BlockSpec((1, tm, D), row), pl.BlockSpec((1, D), fixed),
                  pl.BlockSpec((D, n), fixed), pl.BlockSpec((1, GDN_WIDTH), fixed),
                  pl.BlockSpec((1, GDN_WIDTH), fixed)],
        out_specs=[pl.BlockSpec((1, tm, 3 * GDN_WIDTH), row), pl.BlockSpec((1, tm, GDN_WIDTH), row),
                   pl.BlockSpec((1, tm, GDN_WIDTH), row), pl.BlockSpec((1, tm, GDN_WIDTH), row)],
        out_shape=[jax.ShapeDtypeStruct((B, T, 3 * GDN_WIDTH), F32),
                   jax.ShapeDtypeStruct((B, T, GDN_WIDTH), F32),
                   jax.ShapeDtypeStruct((B, T, GDN_WIDTH), F32),
                   jax.ShapeDtypeStruct((B, T, GDN_WIDTH), F32)],
        compiler_params=_params(("parallel", "parallel")),
        name="gdn_in",
    )(x, gain, w, aneg, dtb)


def _bdot(a, b):
    return jnp.einsum("uik,ukj->uij", a.astype(BF16), b.astype(BF16), preferred_element_type=F32)


def _bdot_nt(a, b):
    return jnp.einsum("uik,ujk->uij", a.astype(BF16), b.astype(BF16), preferred_element_type=F32)


def _gdn_kernel(qkv_ref, la_ref, beta_ref, gate_ref, cw_ref, gn_ref, bd_ref, btri_ref, out_ref, xbuf, state):
    C = CHUNK
    N = GDN_NCH
    H = GDN_HEADS
    R = N * C

    @pl.when(pl.program_id(1) == 0)
    def _():
        xbuf[0:HALO, :] = jnp.zeros((HALO, 3 * GDN_WIDTH), F32)
        state[...] = jnp.zeros(state.shape, F32)

    xbuf[HALO:HALO + R, :] = qkv_ref[0]
    conv = jnp.zeros((R, 3 * GDN_WIDTH), F32)
    for j in range(CONV_K):
        off = HALO - (CONV_K - 1) + j
        conv = conv + cw_ref[j:j + 1, :] * xbuf[off:off + R, :]
    xbuf[0:HALO, :] = xbuf[R:R + HALO, :]
    y = _silu(conv)

    bd = bd_ref[...]
    q_all = y[:, :GDN_WIDTH]
    k_all = y[:, GDN_WIDTH:2 * GDN_WIDTH]
    qs_all = q_all * lax.rsqrt(_dot_sel_right(q_all * q_all, bd) + EPS) * (GDN_DIM ** -0.5)
    kn_all = k_all * lax.rsqrt(_dot_sel_right(k_all * k_all, bd) + EPS)
    g_all = _dot_sel_left(btri_ref[...], la_ref[0])

    def units(arr, col0=0):
        return jnp.stack([arr[c * C:(c + 1) * C, col0 + h * GDN_DIM:col0 + (h + 1) * GDN_DIM]
                          for c in range(N) for h in range(H)])

    qs = units(qs_all)
    kn = units(kn_all)
    v = units(y, 2 * GDN_WIDTH)
    bt = units(beta_ref[0])
    g = units(g_all)

    row = _iota2((C, C), 0)
    col = _iota2((C, C), 1)
    causal = (col <= row)[None]
    strict = (col < row)[None]
    eye = jnp.where(row == col, 1.0, 0.0)[None]
    g_col = g[:, :, :C]
    diff = g_col - jnp.swapaxes(g_col, 1, 2)
    decay = jnp.where(causal, jnp.exp(jnp.where(causal, diff, 0.0)), 0.0)
    eg = jnp.exp(g)
    kb = kn * bt
    a = jnp.where(strict, _bdot_nt(kb, kn) * decay, 0.0)
    t_inv = None
    for lvl in range(1, C.bit_length()):
        same_blk = (row >> lvl) == (col >> lvl)
        lower_left = ((row >> (lvl - 1)) & 1) > ((col >> (lvl - 1)) & 1)
        a_n = jnp.where((same_blk & lower_left)[None], a, 0.0)
        t_inv = eye - a_n if t_inv is None else t_inv - _bdot(_bdot(t_inv, a_n), t_inv)
    w = _bdot(t_inv, kb * eg)
    u_all = _bdot(t_inv, v * bt)
    qk = jnp.where(causal, _bdot_nt(qs, kn) * decay, 0.0)
    qd = qs * eg
    g_last = g[:, C - 1:C, :]
    kd_t = jnp.swapaxes(kn * jnp.exp(g_last - g), 1, 2)
    eg_last = jnp.exp(g_last)

    s = [state[h] for h in range(H)]
    o_rows = []
    for c in range(N):
        o_heads = []
        for h in range(H):
            u = c * H + h
            v_new = u_all[u] - _dot(w[u], s[h])
            o_heads.append(_dot(qd[u], s[h]) + _dot(qk[u], v_new))
            s[h] = s[h] * eg_last[u] + _dot(kd_t[u], v_new)
        o_rows.append(jnp.concatenate(o_heads, axis=1))
    for h in range(H):
        state[h] = s[h]
    o = jnp.concatenate(o_rows, axis=0)
    ms = _dot_sel_right(o * o, bd) * (1.0 / GDN_DIM)
    out_ref[0] = o * lax.rsqrt(ms + EPS) * gn_ref[...] * gate_ref[0]


def _gdn(qkv, la, beta, gate, conv_w, gnorm):
    B, T, _ = qkv.shape
    R = GDN_NCH * CHUNK
    row = lambda b, c: (b, c, 0)
    fixed = lambda b, c: (0, 0)
    idx = jnp.arange(GDN_WIDTH)
    bd = (idx[:, None] // GDN_DIM == idx[None, :] // GDN_DIM).astype(BF16)
    r = jnp.arange(R)
    btri = ((r[:, None] // CHUNK == r[None, :] // CHUNK) & (r[None, :] <= r[:, None])).astype(BF16)
    return pl.pallas_call(
        _gdn_kernel,
        grid=(B, T // R),
        in_specs=[pl.BlockSpec((1, R, 3 * GDN_WIDTH), row), pl.BlockSpec((1, R, GDN_WIDTH), row),
                  pl.BlockSpec((1, R, GDN_WIDTH), row), pl.BlockSpec((1, R, GDN_WIDTH), row),
                  pl.BlockSpec((CONV_K, 3 * GDN_WIDTH), fixed), pl.BlockSpec((1, GDN_WIDTH), fixed),
                  pl.BlockSpec((GDN_WIDTH, GDN_WIDTH), fixed), pl.BlockSpec((R, R), fixed)],
        out_specs=pl.BlockSpec((1, R, GDN_WIDTH), row),
        out_shape=jax.ShapeDtypeStruct((B, T, GDN_WIDTH), F32),
        scratch_shapes=[pltpu.VMEM((HALO + R, 3 * GDN_WIDTH), F32),
                        pltpu.VMEM((GDN_HEADS, GDN_DIM, GDN_DIM), F32)],
        compiler_params=_params(("parallel", "arbitrary")),
        name="gdn",
    )(qkv, la, beta, gate, conv_w, jnp.tile(gnorm.astype(F32), GDN_HEADS)[None, :], bd, btri)


def _fox_in_kernel(x_ref, g_ref, wq_ref, wk_ref, wv_ref, wf_ref, fb_ref, gq_ref, gk_ref, pq_ref, pk_ref,
                   cq_ref, ck_ref, q_ref, k_ref, v_ref, cb_ref, carry):
    tm = FOX_BLK
    i = pl.program_id(1)

    @pl.when(i == 0)
    def _():
        carry[...] = jnp.zeros(carry.shape, F32)

    h = _rms_rows(x_ref[0], g_ref[...]).astype(BF16)
    v_ref[0] = jnp.dot(h, wv_ref[...], preferred_element_type=F32).astype(BF16)
    zf = jnp.dot(h, wf_ref[...], preferred_element_type=F32) + fb_ref[...]
    logf = -_softplus(-zf)
    row = _iota2((tm, tm), 0)
    col = _iota2((tm, tm), 1)
    tri_incl = jnp.where(col <= row, 1.0, 0.0).astype(BF16)
    c_loc = _dot_sel_left(tri_incl, logf * LOG2E)
    cb_ref[0, 0] = jnp.broadcast_to(carry[0:1, :], (8, LANES))
    carry[...] = carry[...] + jnp.broadcast_to(c_loc[tm - 1:tm, :], carry.shape)
    hi, lo = _split2(c_loc)
    hl = jnp.concatenate([hi, lo], axis=1)
    mean_sel = jnp.full((LANES, LANES), 1.0 / FOX_DIM, BF16)
    for hd in range(FOX_HEADS):
        sl = slice(hd * LANES, (hd + 1) * LANES)
        for (w_r, gain_r, place_r, const_r, o_r) in ((wq_ref, gq_ref, pq_ref, cq_ref, q_ref),
                                                    (wk_ref, gk_ref, pk_ref, ck_ref, k_ref)):
            p = jnp.dot(h, w_r[:, sl], preferred_element_type=F32)
            ms = _dot_sel_right(p * p, mean_sel)
            pn = p * lax.rsqrt(ms + EPS) * gain_r[...]
            aug = pn + jnp.dot(hl, place_r[hd], preferred_element_type=F32) + const_r[...]
            o_r[0, hd] = aug.astype(BF16)


def _fox_in(x, gain, wq, wk, wv, wf, fb, gq, gk, pq, pk, cq, ck):
    B, T, D = x.shape
    tm = FOX_BLK
    nb = T // tm
    row = lambda b, i: (b, i, 0)
    fixed2 = lambda b, i: (0, 0)
    fixed3 = lambda b, i: (0, 0, 0)
    hrow = lambda b, i: (b, 0, i, 0)
    return pl.pallas_call(
        _fox_in_kernel,
        grid=(B, nb),
        in_specs=[pl.BlockSpec((1, tm, D), row), pl.BlockSpec((1, D), fixed2),
                  pl.BlockSpec((D, FOX_HEADS * LANES), fixed2), pl.BlockSpec((D, FOX_HEADS * LANES), fixed2),
                  pl.BlockSpec((D, FOX_WIDTH), fixed2), pl.BlockSpec((D, LANES), fixed2),
                  pl.BlockSpec((1, LANES), fixed2), pl.BlockSpec((1, LANES), fixed2),
                  pl.BlockSpec((1, LANES), fixed2),
                  pl.BlockSpec((FOX_HEADS, 2 * LANES, LANES), fixed3),
                  pl.BlockSpec((FOX_HEADS, 2 * LANES, LANES), fixed3),
                  pl.BlockSpec((1, LANES), fixed2), pl.BlockSpec((1, LANES), fixed2)],
        out_specs=[pl.BlockSpec((1, FOX_HEADS, tm, LANES), hrow), pl.BlockSpec((1, FOX_HEADS, tm, LANES), hrow),
                   pl.BlockSpec((1, tm, FOX_WIDTH), row),
                   pl.BlockSpec((1, 1, 8, LANES), lambda b, i: (b, i, 0, 0))],
        out_shape=[jax.ShapeDtypeStruct((B, FOX_HEADS, T, LANES), BF16),
                   jax.ShapeDtypeStruct((B, FOX_HEADS, T, LANES), BF16),
                   jax.ShapeDtypeStruct((B, T, FOX_WIDTH), BF16),
                   jax.ShapeDtypeStruct((B, nb, 8, LANES), F32)],
        scratch_shapes=[pltpu.VMEM((8, LANES), F32)],
        compiler_params=_params(("parallel", "arbitrary")),
        name="fox_in",
    )(x, gain, wq, wk, wv, wf, fb, gq, gk, pq, pk, cq, ck)


def _fox_attn_kernel(q_ref, k_ref, v_ref, cb_ref, o_ref, m_s, l_s, acc_s):
    blk = FOX_BLK
    reps = blk // LANES
    i = pl.program_id(2)
    m_s[...] = jnp.full(m_s.shape, NEG_INF, F32)
    l_s[...] = jnp.zeros(l_s.shape, F32)
    acc_s[...] = jnp.zeros(acc_s.shape, F32)

    def step(j, masked):
        start = pl.multiple_of(j * blk, blk)
        vj = v_ref[0, pl.ds(start, blk), :]
        for hh in range(2):
            kj = k_ref[0, hh, pl.ds(start, blk), :]
            s = _dot_nt(q_ref[0, hh], kj)
            if masked:
                s = jnp.where(_iota2((blk, blk), 1) <= _iota2((blk, blk), 0), s, NEG_INF)
            delta = cb_ref[0, hh, pl.ds(i, 1), :] - cb_ref[0, hh, pl.ds(j, 1), :]
            m_old = m_s[hh]
            m_new = jnp.maximum(m_old, jnp.max(s, axis=-1, keepdims=True) + delta)
            shift = delta - m_new
            p = jnp.exp2(s + jnp.concatenate([shift] * reps, axis=1))
            alpha = jnp.exp2(m_old - m_new)
            l_s[hh] = alpha * l_s[hh] + jnp.sum(p, axis=-1, keepdims=True)
            acc_s[hh] = alpha * acc_s[hh] + jnp.dot(p.astype(BF16), vj, preferred_element_type=F32)
            m_s[hh] = m_new

    def step2(j):
        start0 = pl.multiple_of(j * blk, blk)
        start1 = pl.multiple_of((j + 1) * blk, blk)
        v0 = v_ref[0, pl.ds(start0, blk), :]
        v1 = v_ref[0, pl.ds(start1, blk), :]
        for hh in range(2):
            q = q_ref[0, hh]
            s0 = _dot_nt(q, k_ref[0, hh, pl.ds(start0, blk), :])
            s1 = _dot_nt(q, k_ref[0, hh, pl.ds(start1, blk), :])
            cbi = cb_ref[0, hh, pl.ds(i, 1), :]
            d0 = cbi - cb_ref[0, hh, pl.ds(j, 1), :]
            d1 = cbi - cb_ref[0, hh, pl.ds(j + 1, 1), :]
            m_old = m_s[hh]
            m_new = jnp.maximum(m_old, jnp.maximum(jnp.max(s0, axis=-1, keepdims=True) + d0,
                                                   jnp.max(s1, axis=-1, keepdims=True) + d1))
            p0 = jnp.exp2(s0 + jnp.concatenate([d0 - m_new] * reps, axis=1))
            p1 = jnp.exp2(s1 + jnp.concatenate([d1 - m_new] * reps, axis=1))
            alpha = jnp.exp2(m_old - m_new)
            l_s[hh] = alpha * l_s[hh] + (jnp.sum(p0, axis=-1, keepdims=True) + jnp.sum(p1, axis=-1, keepdims=True))
            pv = (jnp.dot(p0.astype(BF16), v0, preferred_element_type=F32)
                  + jnp.dot(p1.astype(BF16), v1, preferred_element_type=F32))
            acc_s[hh] = alpha * acc_s[hh] + pv
            m_s[hh] = m_new

    def body(k, carry):
        step2(2 * k)
        return carry

    lax.fori_loop(0, i // 2, body, 0)

    @pl.when(i % 2 == 1)
    def _():
        step(i - 1, False)

    step(i, True)
    lane = _iota2((blk, LANES), 1)
    o_ref[0] = jnp.where(lane < FOX_DIM, acc_s[0] / l_s[0], acc_s[1] / l_s[1])


def _fox_attn(q_aug, k_aug, v, cb):
    B, H, T, _ = q_aug.shape
    blk = FOX_BLK
    nb = T // blk
    return pl.pallas_call(
        _fox_attn_kernel,
        grid=(B, H // 2, nb),
        in_specs=[pl.BlockSpec((1, 2, blk, LANES), lambda b, p, i: (b, p, i, 0)),
                  pl.BlockSpec((1, 2, T, LANES), lambda b, p, i: (b, p, 0, 0)),
                  pl.BlockSpec((1, T, LANES), lambda b, p, i: (b, 0, p)),
                  pl.BlockSpec((1, 2, nb, LANES), lambda b, p, i: (b, p, 0, 0))],
        out_specs=pl.BlockSpec((1, blk, LANES), lambda b, p, i: (b, i, p)),
        out_shape=jax.ShapeDtypeStruct((B, T, FOX_WIDTH), F32),
        scratch_shapes=[pltpu.VMEM((2, blk, LANES), F32), pltpu.VMEM((2, blk, LANES), F32),
                        pltpu.VMEM((2, blk, LANES), F32)],
        compiler_params=_params(("parallel", "parallel", "arbitrary")),
        name="fox_attn",
    )(q_aug, k_aug, v, cb)


def _mem_kv_kernel(mem_ref, g_ref, w_ref, gk_ref, k_ref, v_ref):
    h = _rms_rows(mem_ref[0], g_ref[...]).astype(BF16)
    kv = jnp.dot(h, w_ref[...], preferred_element_type=F32)
    for hd in range(MEM_HEADS):
        kh = kv[:, hd * MEM_DIM:(hd + 1) * MEM_DIM]
        k_ref[0, hd] = _rms_rows(kh, gk_ref[...]).astype(BF16)
        v_ref[0, hd] = kv[:, D_MODEL + hd * MEM_DIM:D_MODEL + (hd + 1) * MEM_DIM].astype(BF16)


def _mem_kv(mem, gain, w, gk):
    B, M, D = mem.shape
    fixed = lambda b: (0, 0)
    return pl.pallas_call(
        _mem_kv_kernel,
        grid=(B,),
        in_specs=[pl.BlockSpec((1, M, D), lambda b: (b, 0, 0)), pl.BlockSpec((1, D), fixed),
                  pl.BlockSpec((D, 2 * D), fixed), pl.BlockSpec((1, MEM_DIM), fixed)],
        out_specs=[pl.BlockSpec((1, MEM_HEADS, M, MEM_DIM), lambda b: (b, 0, 0, 0)),
                   pl.BlockSpec((1, MEM_HEADS, M, MEM_DIM), lambda b: (b, 0, 0, 0))],
        out_shape=[jax.ShapeDtypeStruct((B, MEM_HEADS, M, MEM_DIM), BF16),
                   jax.ShapeDtypeStruct((B, MEM_HEADS, M, MEM_DIM), BF16)],
        compiler_params=_params(("parallel",)),
        name="mem_kv",
    )(mem, gain, w, gk)


def _xattn_kernel(x_ref, og_ref, of_ref, wo_ref, g_ref, wq_ref, gq_ref, k_ref, v_ref, wmo_ref, out_ref):
    x1 = (x_ref[0] + _dot(og_ref[0], wo_ref[:GDN_WIDTH, :]) + _dot(of_ref[0], wo_ref[GDN_WIDTH:, :]))
    h = _rms_rows(x1, g_ref[...]).astype(BF16)
    q = jnp.dot(h, wq_ref[...], preferred_element_type=F32)
    heads = []
    for hd in range(MEM_HEADS):
        sl = slice(hd * MEM_DIM, (hd + 1) * MEM_DIM)
        qn = _rms_rows(q[:, sl], gq_ref[...]) * (MEM_DIM ** -0.5)
        s = _dot_nt(qn, k_ref[0, hd])
        m = jnp.max(s, axis=-1, keepdims=True)
        p = jnp.exp(s - m)
        p = p / jnp.sum(p, axis=-1, keepdims=True)
        heads.append(_dot(p, v_ref[0, hd]).astype(BF16))
    o = jnp.concatenate(heads, axis=1)
    out_ref[0] = x1 + jnp.dot(o, wmo_ref[...], preferred_element_type=F32)


def _xattn(x, og, of, wo, gain, wq, gq, k, v, wmo):
    B, T, D = x.shape
    tm = TOK_TILE
    M = k.shape[2]
    row = lambda b, i: (b, i, 0)
    fixed = lambda b, i: (0, 0)
    kvs = lambda b, i: (b, 0, 0, 0)
    return pl.pallas_call(
        _xattn_kernel,
        grid=(B, T // tm),
        in_specs=[pl.BlockSpec((1, tm, D), row), pl.BlockSpec((1, tm, GDN_WIDTH), row),
                  pl.BlockSpec((1, tm, FOX_WIDTH), row), pl.BlockSpec((D, D), fixed),
                  pl.BlockSpec((1, D), fixed), pl.BlockSpec((D, D), fixed), pl.BlockSpec((1, MEM_DIM), fixed),
                  pl.BlockSpec((1, MEM_HEADS, M, MEM_DIM), kvs), pl.BlockSpec((1, MEM_HEADS, M, MEM_DIM), kvs),
                  pl.BlockSpec((D, D), fixed)],
        out_specs=pl.BlockSpec((1, tm, D), row),
        out_shape=jax.ShapeDtypeStruct((B, T, D), F32),
        compiler_params=_params(("parallel", "parallel")),
        name="xattn",
    )(x, og, of, wo, gain, wq, gq, k, v, wmo)


def _sort_network(n):
    out = []
    p = 1
    while p < n:
        k = p
        while k >= 1:
            for j in range(k % p, n - k, 2 * k):
                for i in range(min(k, n - j - k)):
                    if (i + j) // (2 * p) == (i + j + k) // (2 * p):
                        out.append((i + j, i + j + k))
            k //= 2
        p *= 2
    return out


def _top_values(s, n, rows_ref):
    g = 8
    lists = [s[k * g:(k + 1) * g, :] for k in range(s.shape[0] // g)]
    for i, j in _sort_network(len(lists)):
        hi = jnp.maximum(lists[i], lists[j])
        lists[j] = jnp.minimum(lists[i], lists[j])
        lists[i] = hi
    lists.append(jnp.full(lists[0].shape, NEG_INF, F32))
    for r in range(n):
        m = jnp.max(lists[0], axis=0, keepdims=True)
        rows_ref[r:r + 1, :] = m
        popped = lists[0] == m
        for k in range(min(n - 1 - r, len(lists) - 1)):
            lists[k] = jnp.where(popped, lists[k + 1], lists[k])


def _count_rows(s, rows, out_ref, below):
    g = 16
    for k in range(s.shape[0] // g):
        sg = s[k * g:(k + 1) * g, :]
        cnt = jnp.zeros(sg.shape, F32)
        for r, row in enumerate(rows):
            cnt = jnp.where(sg < row if below else sg >= row, float(r + 1), cnt)
        out_ref[k * g:(k + 1) * g, :] = cnt.astype(out_ref.dtype)


PEER_NSEL = PEER_TOPK + 1
PEER_PAIRS = [(i, j) for i in range(PEER_NSEL) for j in range(PEER_NSEL) if (i + 1) * (j + 1) <= PEER_NSEL]
PEER_CAND_ROWS = 64
assert len(PEER_PAIRS) <= PEER_CAND_ROWS
PEER_TOP_ROWS = -(-PEER_NSEL // 8) * 8


def _peer_kernel(x_ref, g_ref, wqt_ref, keys_ref, u_ref, vt_ref, out_ref,
                 h_s, n_s, e1_s, e2_s, rank_s, top_s, cand_s, acc_s, sc_s, w_s):
    tm = PEER_TOK
    j = pl.program_id(1)
    nj = pl.num_programs(1)

    @pl.when(j == 0)
    def _():
        h = _rms_rows(x_ref[...], g_ref[...]).astype(BF16)
        h_s[...] = h
        acc_s[...] = jnp.zeros(acc_s.shape, F32)
        cand_s[...] = jnp.full(cand_s.shape, NEG_INF, F32)
        for hd in range(PEER_HEADS):
            scores = []
            for half in range(2):
                r0 = (hd * 2 + half) * PEER_HALF
                qt = _dot_nt(wqt_ref[r0:r0 + PEER_HALF, :], h)
                scores.append(_dot(keys_ref[half, hd], qt))
            s1, s2 = scores
            _top_values(s1, PEER_NSEL, top_s.at[0])
            _top_values(s2, PEER_NSEL, top_s.at[1])
            _count_rows(s2, [top_s[1, r:r + 1, :] for r in range(PEER_NSEL)], rank_s.at[hd], below=True)
            for r, (i1, i2) in enumerate(PEER_PAIRS):
                cand_s[r:r + 1, :] = top_s[0, i1:i1 + 1, :] + top_s[1, i2:i2 + 1, :]
            cand = cand_s[...]
            _top_values(cand, PEER_NSEL, top_s.at[2])
            tau = 0.5 * (top_s[2, PEER_TOPK - 1:PEER_TOPK, :] + top_s[2, PEER_TOPK:PEER_TOPK + 1, :])
            v1max = top_s[0, 0:1, :]
            v2max = top_s[1, 0:1, :]
            z = jnp.sum(jnp.where(cand >= tau, jnp.exp(cand - (v1max + v2max)), 0.0), axis=0, keepdims=True)
            e1_s[hd] = jnp.exp(s1 - v1max)
            e2_s[hd] = (jnp.exp(s2 - v2max) / z).astype(BF16)
            _count_rows(s1, [tau - top_s[1, r:r + 1, :] for r in range(PEER_TOPK)], n_s.at[hd], below=False)

    hb = h_s[...]
    n_sub = PEER_ECHUNK // PEER_SUB
    a_per_sub = PEER_SUB // PEER_KEYS
    rep = PEER_KEYS // 16

    def score(sub):
        sc_s[sub % 2] = _dot_nt(u_ref[sub * PEER_SUB:(sub + 1) * PEER_SUB, :], hb)

    score(0)
    for sub in range(n_sub):
        slot = sub % 2
        if sub + 1 < n_sub:
            score(sub + 1)
        for ai in range(a_per_sub):
            a_idx = j * (PEER_ECHUNK // PEER_KEYS) + sub * a_per_sub + ai
            ar = slice(ai * PEER_KEYS, (ai + 1) * PEER_KEYS)
            gate = None
            for hd in range(PEER_HEADS):
                n16 = jnp.broadcast_to(n_s[hd, pl.ds(a_idx, 1), :], (16, tm)).astype(BF16)
                e16 = jnp.broadcast_to(e1_s[hd, pl.ds(a_idx, 1), :], (16, tm)).astype(BF16)
                n_a = jnp.concatenate([n16] * rep, axis=0)
                e1_a = jnp.concatenate([e16] * rep, axis=0)
                term = jnp.where(rank_s[hd] < n_a, e1_a * e2_s[hd], jnp.zeros((), BF16))
                gate = term if gate is None else gate + term
            w_s[slot, ar, :] = _gelu_tanh(sc_s[slot, ar, :]).astype(BF16) * gate
        rows = slice(sub * PEER_SUB, (sub + 1) * PEER_SUB)
        acc_s[...] += jnp.dot(vt_ref[:, rows], w_s[slot], preferred_element_type=F32)

    @pl.when(j == nj - 1)
    def _():
        out_ref[...] = x_ref[...] + acc_s[...].T


def _peer(x2d, gain, wqt, keys, u, vt):
    M, D = x2d.shape
    tm = PEER_TOK
    E = u.shape[0]
    ec = PEER_ECHUNK
    return pl.pallas_call(
        _peer_kernel,
        grid=(M // tm, E // ec),
        in_specs=[pl.BlockSpec((tm, D), lambda i, j: (i, 0)), pl.BlockSpec((1, D), lambda i, j: (0, 0)),
                  pl.BlockSpec((PEER_HEADS * 2 * PEER_HALF, D), lambda i, j: (0, 0)),
                  pl.BlockSpec((2, PEER_HEADS, PEER_KEYS, PEER_HALF), lambda i, j: (0, 0, 0, 0)),
                  pl.BlockSpec((ec, D), lambda i, j: (j, 0)), pl.BlockSpec((D, ec), lambda i, j: (0, j))],
        out_specs=pl.BlockSpec((tm, D), lambda i, j: (i, 0)),
        out_shape=jax.ShapeDtypeStruct((M, D), F32),
        scratch_shapes=[pltpu.VMEM((tm, D), BF16),
                        pltpu.VMEM((PEER_HEADS, PEER_KEYS, tm), F32),
                        pltpu.VMEM((PEER_HEADS, PEER_KEYS, tm), F32),
                        pltpu.VMEM((PEER_HEADS, PEER_KEYS, tm), BF16),
                        pltpu.VMEM((PEER_HEADS, PEER_KEYS, tm), BF16),
                        pltpu.VMEM((3, PEER_TOP_ROWS, tm), F32),
                        pltpu.VMEM((PEER_CAND_ROWS, tm), F32),
                        pltpu.VMEM((D, tm), F32),
                        pltpu.VMEM((2, PEER_SUB, tm), F32),
                        pltpu.VMEM((2, PEER_SUB, tm), BF16)],
        compiler_params=_params(("parallel", "arbitrary")),
        name="peer",
    )(x2d, gain, wqt, keys, u, vt)


def _lane_rep(v, width):
    return jnp.repeat(v.astype(F32), width)[None, :]


def _fox_layout(w_heads):
    D = w_heads.shape[0]
    out = jnp.zeros((D, FOX_HEADS, LANES), F32)
    out = out.at[:, :, :FOX_DIM].set(w_heads.reshape(D, FOX_HEADS, FOX_DIM))
    return out.reshape(D, FOX_HEADS * LANES).astype(BF16)


def _placement(lane_hi, lane_lo, sign):
    p = jnp.zeros((FOX_HEADS, 2 * LANES, LANES), F32)
    hs = jnp.arange(FOX_HEADS)
    p = p.at[hs, hs, lane_hi].set(sign)
    p = p.at[hs, LANES + hs, lane_lo].set(sign)
    return p.astype(BF16)


def _lane_const(lanes):
    c = jnp.zeros((1, LANES), F32)
    return c.at[0, jnp.array(lanes)].set(1.0)


def _pad_lanes(v, scale=1.0):
    return jnp.zeros((1, LANES), F32).at[0, :v.shape[0]].set(v.astype(F32) * scale)


def kernel(x, mem, mix_norm, w_in, conv_w, a_log, dt_bias, gdn_out_norm, fox_q_norm, fox_k_norm, fox_f_bias,
           w_out, xattn_norm, mem_norm, w_mq, w_mkv, mq_norm, mk_norm, w_mo, ffn_norm, w_pq, sub_keys,
           expert_u, expert_v):
    B, T, D = x.shape
    depth = w_in.shape[0]
    o1 = 3 * GDN_WIDTH
    o2 = o1 + GDN_WIDTH
    o3 = o2 + GDN_HEADS
    o4 = o3 + GDN_HEADS
    o5 = o4 + 3 * FOX_WIDTH
    nb = T // FOX_BLK
    place_q = _placement(FOX_DIM, FOX_DIM + 1, 1.0)
    place_k = _placement(FOX_DIM + 2, FOX_DIM + 3, -1.0)
    const_q = _lane_const([FOX_DIM + 2, FOX_DIM + 3])
    const_k = _lane_const([FOX_DIM, FOX_DIM + 1])

    for l in range(depth):
        wi = w_in[l]
        w_gdn = jnp.concatenate([wi[:, :o2], jnp.repeat(wi[:, o2:o3], GDN_DIM, axis=1),
                                 jnp.repeat(wi[:, o3:o4], GDN_DIM, axis=1)], axis=1).astype(BF16)
        aneg = _lane_rep(-jnp.exp(a_log[l].astype(F32)), GDN_DIM)
        dtb = _lane_rep(dt_bias[l], GDN_DIM)
        qkv, gate, la, beta = _gdn_in(x, mix_norm[l][None, :], w_gdn, aneg, dtb)
        o_gdn = _gdn(qkv, la, beta, gate, conv_w[l].reshape(CONV_K, 3 * GDN_WIDTH),
                     gdn_out_norm[l])
        fq_w = _fox_layout(wi[:, o4:o4 + FOX_WIDTH])
        fk_w = _fox_layout(wi[:, o4 + FOX_WIDTH:o4 + 2 * FOX_WIDTH])
        fv_w = wi[:, o4 + 2 * FOX_WIDTH:o5].astype(BF16)
        ff_w = jnp.zeros((D, LANES), F32).at[:, :FOX_HEADS].set(wi[:, o5:]).astype(BF16)
        q_aug, k_aug, v_fox, cb = _fox_in(
            x, mix_norm[l][None, :], fq_w, fk_w, fv_w, ff_w, _pad_lanes(fox_f_bias[l]),
            _pad_lanes(fox_q_norm[l], FOX_DIM ** -0.5 * LOG2E), _pad_lanes(fox_k_norm[l]),
            place_q, place_k, const_q, const_k)
        cb_heads = jnp.broadcast_to(
            jnp.transpose(cb[:, :, 0, :FOX_HEADS], (0, 2, 1))[..., None], (B, FOX_HEADS, nb, LANES))
        o_fox = _fox_attn(q_aug, k_aug, v_fox, cb_heads)
        k_mem, v_mem = _mem_kv(mem, mem_norm[l][None, :], w_mkv[l].astype(BF16), mk_norm[l][None, :])
        x = _xattn(x, o_gdn, o_fox, w_out[l].astype(BF16), xattn_norm[l][None, :], w_mq[l].astype(BF16),
                   mq_norm[l][None, :], k_mem, v_mem, w_mo[l].astype(BF16))
        x = _peer(x.reshape(B * T, D), ffn_norm[l][None, :], w_pq[l].T.astype(BF16),
                  sub_keys[l].astype(BF16), expert_u[l].astype(BF16),
                  expert_v[l].T.astype(BF16)).reshape(B, T, D)
    return x
```

```python
import functools

import jax
import jax.numpy as jnp
from jax import lax
from jax.experimental import pallas as pl
from jax.experimental.pallas import tpu as pltpu

F32 = jnp.float32
BF16 = jnp.bfloat16
EPS = 1e-6
NEG_INF = float("-inf")
LOG2E = 1.4426950408889634

D_MODEL = 1024
CHUNK = 64
GDN_HEADS = 4
GDN_DIM = 128
GDN_WIDTH = GDN_HEADS * GDN_DIM
CONV_K = 4
FOX_HEADS = 8
FOX_DIM = 64
FOX_WIDTH = FOX_HEADS * FOX_DIM
MEM_HEADS = 4
MEM_DIM = 256
PEER_HEADS = 8
PEER_KEYS = 128
PEER_HALF = 128
PEER_TOPK = 16
PEER_EXPERTS = PEER_KEYS * PEER_KEYS
LANES = 128
HALO = 8
GDN_NCH = 4

TOK_TILE = 512
FOX_BLK = 512
PEER_TOK = 512
PEER_ECHUNK = 2048
PEER_SUB = 512
VMEM_LIMIT = 56 * 1024 * 1024


def _dot(a, b):
    return jnp.dot(a.astype(BF16), b.astype(BF16), preferred_element_type=F32)


def _dot_nt(a, b):
    return lax.dot_general(a.astype(BF16), b.astype(BF16), (((1,), (1,)), ((), ())),
                           preferred_element_type=F32)


def _split2(x):
    hi = x.astype(BF16)
    lo = (x - hi.astype(F32)).astype(BF16)
    return hi, lo


def _split3(x):
    hi = x.astype(BF16)
    r = x - hi.astype(F32)
    mid = r.astype(BF16)
    lo = (r - mid.astype(F32)).astype(BF16)
    return hi, mid, lo


def _dot_sel_left(sel, x):
    hi, mid, lo = _split3(x)
    d = functools.partial(jnp.dot, preferred_element_type=F32)
    return d(sel, hi) + d(sel, mid) + d(sel, lo)


def _dot_sel_right(x, sel, pieces=2):
    d = functools.partial(jnp.dot, preferred_element_type=F32)
    if pieces == 2:
        hi, lo = _split2(x)
        return d(hi, sel) + d(lo, sel)
    hi, mid, lo = _split3(x)
    return d(hi, sel) + d(mid, sel) + d(lo, sel)


def _sigmoid(x):
    return 1.0 / (1.0 + jnp.exp(-x))


def _silu(x):
    return x * _sigmoid(x)


def _softplus(x):
    return jnp.maximum(x, 0.0) + jnp.log(1.0 + jnp.exp(-jnp.abs(x)))


def _gelu_tanh(x):
    c = 0.7978845608028654
    k0 = -2.0 * c * LOG2E
    k1 = k0 * 0.044715
    return x / (1.0 + jnp.exp2(x * (k0 + k1 * (x * x))))


def _rms_rows(x, gain):
    ms = jnp.mean(x * x, axis=-1, keepdims=True)
    return x * lax.rsqrt(ms + EPS) * gain


def _iota2(shape, dim):
    return lax.broadcasted_iota(jnp.int32, shape, dim)


def _params(sem):
    return pltpu.CompilerParams(dimension_semantics=sem, vmem_limit_bytes=VMEM_LIMIT)


def _gdn_in_kernel(x_ref, g_ref, w_ref, aneg_ref, dtb_ref, qkv_ref, gate_ref, la_ref, beta_ref):
    h = _rms_rows(x_ref[0], g_ref[...]).astype(BF16)
    o_gate = 3 * GDN_WIDTH
    o_a = o_gate + GDN_WIDTH
    o_b = o_a + GDN_WIDTH
    qkv_ref[0] = jnp.dot(h, w_ref[:, :o_gate], preferred_element_type=F32)
    gate_ref[0] = _silu(jnp.dot(h, w_ref[:, o_gate:o_a], preferred_element_type=F32))
    a = jnp.dot(h, w_ref[:, o_a:o_b], preferred_element_type=F32)
    la_ref[0] = aneg_ref[...] * _softplus(a + dtb_ref[...])
    b = jnp.dot(h, w_ref[:, o_b:], preferred_element_type=F32)
    beta_ref[0] = _sigmoid(b)


def _gdn_in(x, gain, w, aneg, dtb):
    B, T, D = x.shape
    n = w.shape[1]
    tm = TOK_TILE
    row = lambda b, i: (b, i, 0)
    fixed = lambda b, i: (0, 0)
    return pl.pallas_call(
        _gdn_in_kernel,
        grid=(B, T // tm),
        in_specs=[pl.BlockSpec((1, tm, D), row), pl.BlockSpec((1, D), fixed),
                  pl.BlockSpec((D, n), fixed), pl.BlockSpec((1, GDN_WIDTH), fixed),
                  pl.BlockSpec((1, GDN_WIDTH), fixed)],
        out_specs=[pl.BlockSpec((1, tm, 3 * GDN_WIDTH), row), pl.BlockSpec((1, tm, GDN_WIDTH), row),
                   pl.BlockSpec((1, tm, GDN_WIDTH), row), pl.BlockSpec((1, tm, GDN_WIDTH), row)],
        out_shape=[jax.ShapeDtypeStruct((B, T, 3 * GDN_WIDTH), F32),
                   jax.ShapeDtypeStruct((B, T, GDN_WIDTH), F32),
                   jax.ShapeDtypeStruct((B, T, GDN_WIDTH), F32),
                   jax.ShapeDtypeStruct((B, T, GDN_WIDTH), F32)],
        compiler_params=_params(("parallel", "parallel")),
        name="gdn_in",
    )(x, gain, w, aneg, dtb)


def _bdot(a, b):
    return jnp.einsum("uik,ukj->uij", a.astype(BF16), b.astype(BF16), preferred_element_type=F32)


def _bdot_nt(a, b):
    return jnp.einsum("uik,ujk->uij", a.astype(BF16), b.astype(BF16), preferred_element_type=F32)


def _gdn_kernel(qkv_ref, la_ref, beta_ref, gate_ref, cw_ref, gn_ref, bd_ref, btri_ref, out_ref, xbuf, state):
    C = CHUNK
    N = GDN_NCH
    H = GDN_HEADS
    R = N * C

    @pl.when(pl.program_id(1) == 0)
    def _():
        xbuf[0:HALO, :] = jnp.zeros((HALO, 3 * GDN_WIDTH), F32)
        state[...] = jnp.zeros(state.shape, F32)

    xbuf[HALO:HALO + R, :] = qkv_ref[0]
    conv = jnp.zeros((R, 3 * GDN_WIDTH), F32)
    for j in range(CONV_K):
        off = HALO - (CONV_K - 1) + j
        conv = conv + cw_ref[j:j + 1, :] * xbuf[off:off + R, :]
    xbuf[0:HALO, :] = xbuf[R:R + HALO, :]
    y = _silu(conv)

    bd = bd_ref[...]
    q_all = y[:, :GDN_WIDTH]
    k_all = y[:, GDN_WIDTH:2 * GDN_WIDTH]
    qs_all = q_all * lax.rsqrt(_dot_sel_right(q_all * q_all, bd) + EPS) * (GDN_DIM ** -0.5)
    kn_all = k_all * lax.rsqrt(_dot_sel_right(k_all * k_all, bd) + EPS)
    g_all = _dot_sel_left(btri_ref[...], la_ref[0])

    def units(arr, col0=0):
        return jnp.stack([arr[c * C:(c + 1) * C, col0 + h * GDN_DIM:col0 + (h + 1) * GDN_DIM]
                          for c in range(N) for h in range(H)])

    qs = units(qs_all)
    kn = units(kn_all)
    v = units(y, 2 * GDN_WIDTH)
    bt = units(beta_ref[0])
    g = units(g_all)

    row = _iota2((C, C), 0)
    col = _iota2((C, C), 1)
    causal = (col <= row)[None]
    strict = (col < row)[None]
    eye = jnp.where(row == col, 1.0, 0.0)[None]
    g_col = g[:, :, :C]
    diff = g_col - jnp.swapaxes(g_col, 1, 2)
    decay = jnp.where(causal, jnp.exp(jnp.where(causal, diff, 0.0)), 0.0)
    eg = jnp.exp(g)
    kb = kn * bt
    a = jnp.where(strict, _bdot_nt(kb, kn) * decay, 0.0)
    t_inv = None
    for lvl in range(1, C.bit_length()):
        same_blk = (row >> lvl) == (col >> lvl)
        lower_left = ((row >> (lvl - 1)) & 1) > ((col >> (lvl - 1)) & 1)
        a_n = jnp.where((same_blk & lower_left)[None], a, 0.0)
        t_inv = eye - a_n if t_inv is None else t_inv - _bdot(_bdot(t_inv, a_n), t_inv)
    w = _bdot(t_inv, kb * eg)
    u_all = _bdot(t_inv, v * bt)
    qk = jnp.where(causal, _bdot_nt(qs, kn) * decay, 0.0)
    qd = qs * eg
    g_last = g[:, C - 1:C, :]
    kd_t = jnp.swapaxes(kn * jnp.exp(g_last - g), 1, 2)
    eg_last = jnp.exp(g_last)

    s = [state[h] for h in range(H)]
    o_rows = []
    for c in range(N):
        o_heads = []
        for h in range(H):
            u = c * H + h
            v_new = u_all[u] - _dot(w[u], s[h])
            o_heads.append(_dot(qd[u], s[h]) + _dot(qk[u], v_new))
            s[h] = s[h] * eg_last[u] + _dot(kd_t[u], v_new)
        o_rows.append(jnp.concatenate(o_heads, axis=1))
    for h in range(H):
        state[h] = s[h]
    o = jnp.concatenate(o_rows, axis=0)
    ms = _dot_sel_right(o * o, bd) * (1.0 / GDN_DIM)
    out_ref[0] = o * lax.rsqrt(ms + EPS) * gn_ref[...] * gate_ref[0]


def _gdn(qkv, la, beta, gate, conv_w, gnorm):
    B, T, _ = qkv.shape
    R = GDN_NCH * CHUNK
    row = lambda b, c: (b, c, 0)
    fixed = lambda b, c: (0, 0)
    idx = jnp.arange(GDN_WIDTH)
    bd = (idx[:, None] // GDN_DIM == idx[None, :] // GDN_DIM).astype(BF16)
    r = jnp.arange(R)
    btri = ((r[:, None] // CHUNK == r[None, :] // CHUNK) & (r[None, :] <= r[:, None])).astype(BF16)
    return pl.pallas_call(
        _gdn_kernel,
        grid=(B, T // R),
        in_specs=[pl.BlockSpec((1, R, 3 * GDN_WIDTH), row), pl.BlockSpec((1, R, GDN_WIDTH), row),
                  pl.BlockSpec((1, R, GDN_WIDTH), row), pl.BlockSpec((1, R, GDN_WIDTH), row),
                  pl.BlockSpec((CONV_K, 3 * GDN_WIDTH), fixed), pl.BlockSpec((1, GDN_WIDTH), fixed),
                  pl.BlockSpec((GDN_WIDTH, GDN_WIDTH), fixed), pl.BlockSpec((R, R), fixed)],
        out_specs=pl.BlockSpec((1, R, GDN_WIDTH), row),
        out_shape=jax.ShapeDtypeStruct((B, T, GDN_WIDTH), F32),
        scratch_shapes=[pltpu.VMEM((HALO + R, 3 * GDN_WIDTH), F32),
                        pltpu.VMEM((GDN_HEADS, GDN_DIM, GDN_DIM), F32)],
        compiler_params=_params(("parallel", "arbitrary")),
        name="gdn",
    )(qkv, la, beta, gate, conv_w, jnp.tile(gnorm.astype(F32), GDN_HEADS)[None, :], bd, btri)


def _fox_in_kernel(x_ref, g_ref, wq_ref, wk_ref, wv_ref, wf_ref, fb_ref, gq_ref, gk_ref, pq_ref, pk_ref,
                   cq_ref, ck_ref, q_ref, k_ref, v_ref, cb_ref, carry):
    tm = FOX_BLK
    i = pl.program_id(1)

    @pl.when(i == 0)
    def _():
        carry[...] = jnp.zeros(carry.shape, F32)

    h = _rms_rows(x_ref[0], g_ref[...]).astype(BF16)
    v_ref[0] = jnp.dot(h, wv_ref[...], preferred_element_type=F32).astype(BF16)
    zf = jnp.dot(h, wf_ref[...], preferred_element_type=F32) + fb_ref[...]
    logf = -_softplus(-zf)
    row = _iota2((tm, tm), 0)
    col = _iota2((tm, tm), 1)
    tri_incl = jnp.where(col <= row, 1.0, 0.0).astype(BF16)
    c_loc = _dot_sel_left(tri_incl, logf * LOG2E)
    cb_ref[0, 0] = jnp.broadcast_to(carry[0:1, :], (8, LANES))
    carry[...] = carry[...] + jnp.broadcast_to(c_loc[tm - 1:tm, :], carry.shape)
    hi, lo = _split2(c_loc)
    hl = jnp.concatenate([hi, lo], axis=1)
    for (w_r, gain_r, place_r, const_r, o_r) in ((wq_ref, gq_ref, pq_ref, cq_ref, q_ref),
                                                (wk_ref, gk_ref, pk_ref, ck_ref, k_ref)):
        proj = jnp.dot(h, w_r[...], preferred_element_type=F32)
        place = jnp.dot(hl, place_r[...], preferred_element_type=F32)
        for hd in range(FOX_HEADS):
            sl = slice(hd * LANES, (hd + 1) * LANES)
            p = proj[:, sl]
            ms = jnp.sum(p * p, axis=-1, keepdims=True) * (1.0 / FOX_DIM)
            aug = p * lax.rsqrt(ms + EPS) * gain_r[...] + place[:, sl] + const_r[...]
            o_r[0, hd] = aug.astype(BF16)


def _fox_in(x, gain, wq, wk, wv, wf, fb, gq, gk, pq, pk, cq, ck):
    B, T, D = x.shape
    tm = FOX_BLK
    nb = T // tm
    row = lambda b, i: (b, i, 0)
    fixed2 = lambda b, i: (0, 0)
    hrow = lambda b, i: (b, 0, i, 0)
    return pl.pallas_call(
        _fox_in_kernel,
        grid=(B, nb),
        in_specs=[pl.BlockSpec((1, tm, D), row), pl.BlockSpec((1, D), fixed2),
                  pl.BlockSpec((D, FOX_HEADS * LANES), fixed2), pl.BlockSpec((D, FOX_HEADS * LANES), fixed2),
                  pl.BlockSpec((D, FOX_WIDTH), fixed2), pl.BlockSpec((D, LANES), fixed2),
                  pl.BlockSpec((1, LANES), fixed2), pl.BlockSpec((1, LANES), fixed2),
                  pl.BlockSpec((1, LANES), fixed2),
                  pl.BlockSpec((2 * LANES, FOX_HEADS * LANES), fixed2),
                  pl.BlockSpec((2 * LANES, FOX_HEADS * LANES), fixed2),
                  pl.BlockSpec((1, LANES), fixed2), pl.BlockSpec((1, LANES), fixed2)],
        out_specs=[pl.BlockSpec((1, FOX_HEADS, tm, LANES), hrow), pl.BlockSpec((1, FOX_HEADS, tm, LANES), hrow),
                   pl.BlockSpec((1, tm, FOX_WIDTH), row),
                   pl.BlockSpec((1, 1, 8, LANES), lambda b, i: (b, i, 0, 0))],
        out_shape=[jax.ShapeDtypeStruct((B, FOX_HEADS, T, LANES), BF16),
                   jax.ShapeDtypeStruct((B, FOX_HEADS, T, LANES), BF16),
                   jax.ShapeDtypeStruct((B, T, FOX_WIDTH), BF16),
                   jax.ShapeDtypeStruct((B, nb, 8, LANES), F32)],
        scratch_shapes=[pltpu.VMEM((8, LANES), F32)],
        compiler_params=_params(("parallel", "arbitrary")),
        name="fox_in",
    )(x, gain, wq, wk, wv, wf, fb, gq, gk, pq, pk, cq, ck)


def _fox_attn_kernel(q_ref, k_ref, v_ref, cb_ref, o_ref, m_s, l_s, acc_s):
    blk = FOX_BLK
    reps = blk // LANES
    i = pl.program_id(2)
    m_s[...] = jnp.full(m_s.shape, NEG_INF, F32)
    l_s[...] = jnp.zeros(l_s.shape, F32)
    acc_s[...] = jnp.zeros(acc_s.shape, F32)

    def step(j, masked):
        start = pl.multiple_of(j * blk, blk)
        vj = v_ref[0, pl.ds(start, blk), :]
        for hh in range(2):
            kj = k_ref[0, hh, pl.ds(start, blk), :]
            s = _dot_nt(q_ref[0, hh], kj)
            if masked:
                s = jnp.where(_iota2((blk, blk), 1) <= _iota2((blk, blk), 0), s, NEG_INF)
            delta = cb_ref[0, hh, pl.ds(i, 1), :] - cb_ref[0, hh, pl.ds(j, 1), :]
            m_old = m_s[hh]
            m_new = jnp.maximum(m_old, jnp.max(s, axis=-1, keepdims=True) + delta)
            shift = delta - m_new
            p = jnp.exp2(s + jnp.concatenate([shift] * reps, axis=1))
            alpha = jnp.exp2(m_old - m_new)
            l_s[hh] = alpha * l_s[hh] + jnp.sum(p, axis=-1, keepdims=True)
            acc_s[hh] = alpha * acc_s[hh] + jnp.dot(p.astype(BF16), vj, preferred_element_type=F32)
            m_s[hh] = m_new

    def step2(j):
        start0 = pl.multiple_of(j * blk, blk)
        start1 = pl.multiple_of((j + 1) * blk, blk)
        v0 = v_ref[0, pl.ds(start0, blk), :]
        v1 = v_ref[0, pl.ds(start1, blk), :]
        for hh in range(2):
            q = q_ref[0, hh]
            s0 = _dot_nt(q, k_ref[0, hh, pl.ds(start0, blk), :])
            s1 = _dot_nt(q, k_ref[0, hh, pl.ds(start1, blk), :])
            cbi = cb_ref[0, hh, pl.ds(i, 1), :]
            d0 = cbi - cb_ref[0, hh, pl.ds(j, 1), :]
            d1 = cbi - cb_ref[0, hh, pl.ds(j + 1, 1), :]
            m_old = m_s[hh]
            m_new = jnp.maximum(m_old, jnp.maximum(jnp.max(s0, axis=-1, keepdims=True) + d0,
                                                   jnp.max(s1, axis=-1, keepdims=True) + d1))
            p0 = jnp.exp2(s0 + jnp.concatenate([d0 - m_new] * reps, axis=1))
            p1 = jnp.exp2(s1 + jnp.concatenate([d1 - m_new] * reps, axis=1))
            alpha = jnp.exp2(m_old - m_new)
            l_s[hh] = alpha * l_s[hh] + (jnp.sum(p0, axis=-1, keepdims=True) + jnp.sum(p1, axis=-1, keepdims=True))
            pv = (jnp.dot(p0.astype(BF16), v0, preferred_element_type=F32)
                  + jnp.dot(p1.astype(BF16), v1, preferred_element_type=F32))
            acc_s[hh] = alpha * acc_s[hh] + pv
            m_s[hh] = m_new

    def body(k, carry):
        step2(2 * k)
        return carry

    lax.fori_loop(0, i // 2, body, 0)

    @pl.when(i % 2 == 1)
    def _():
        step(i - 1, False)

    step(i, True)
    lane = _iota2((blk, LANES), 1)
    o_ref[0] = jnp.where(lane < FOX_DIM, acc_s[0] / l_s[0], acc_s[1] / l_s[1])


def _fox_attn(q_aug, k_aug, v, cb):
    B, H, T, _ = q_aug.shape
    blk = FOX_BLK
    nb = T // blk
    return pl.pallas_call(
        _fox_attn_kernel,
        grid=(B, H // 2, nb),
        in_specs=[pl.BlockSpec((1, 2, blk, LANES), lambda b, p, i: (b, p, i, 0)),
                  pl.BlockSpec((1, 2, T, LANES), lambda b, p, i: (b, p, 0, 0)),
                  pl.BlockSpec((1, T, LANES), lambda b, p, i: (b, 0, p)),
                  pl.BlockSpec((1, 2, nb, LANES), lambda b, p, i: (b, p, 0, 0))],
        out_specs=pl.BlockSpec((1, blk, LANES), lambda b, p, i: (b, i, p)),
        out_shape=jax.ShapeDtypeStruct((B, T, FOX_WIDTH), F32),
        scratch_shapes=[pltpu.VMEM((2, blk, LANES), F32), pltpu.VMEM((2, blk, LANES), F32),
                        pltpu.VMEM((2, blk, LANES), F32)],
        compiler_params=_params(("parallel", "parallel", "arbitrary")),
        name="fox_attn",
    )(q_aug, k_aug, v, cb)


def _mem_kv_kernel(mem_ref, g_ref, w_ref, gk_ref, k_ref, v_ref):
    h = _rms_rows(mem_ref[0], g_ref[...]).astype(BF16)
    kv = jnp.dot(h, w_ref[...], preferred_element_type=F32)
    for hd in range(MEM_HEADS):
        kh = kv[:, hd * MEM_DIM:(hd + 1) * MEM_DIM]
        k_ref[0, hd] = _rms_rows(kh, gk_ref[...]).astype(BF16)
        v_ref[0, hd] = kv[:, D_MODEL + hd * MEM_DIM:D_MODEL + (hd + 1) * MEM_DIM].astype(BF16)


def _mem_kv(mem, gain, w, gk):
    B, M, D = mem.shape
    fixed = lambda b: (0, 0)
    return pl.pallas_call(
        _mem_kv_kernel,
        grid=(B,),
        in_specs=[pl.BlockSpec((1, M, D), lambda b: (b, 0, 0)), pl.BlockSpec((1, D), fixed),
                  pl.BlockSpec((D, 2 * D), fixed), pl.BlockSpec((1, MEM_DIM), fixed)],
        out_specs=[pl.BlockSpec((1, MEM_HEADS, M, MEM_DIM), lambda b: (b, 0, 0, 0)),
                   pl.BlockSpec((1, MEM_HEADS, M, MEM_DIM), lambda b: (b, 0, 0, 0))],
        out_shape=[jax.ShapeDtypeStruct((B, MEM_HEADS, M, MEM_DIM), BF16),
                   jax.ShapeDtypeStruct((B, MEM_HEADS, M, MEM_DIM), BF16)],
        compiler_params=_params(("parallel",)),
        name="mem_kv",
    )(mem, gain, w, gk)


def _xattn_kernel(x_ref, og_ref, of_ref, wo_ref, g_ref, wq_ref, gq_ref, k_ref, v_ref, wmo_ref, out_ref):
    x1 = (x_ref[0] + _dot(og_ref[0], wo_ref[:GDN_WIDTH, :]) + _dot(of_ref[0], wo_ref[GDN_WIDTH:, :]))
    h = _rms_rows(x1, g_ref[...]).astype(BF16)
    q = jnp.dot(h, wq_ref[...], preferred_element_type=F32)
    heads = []
    for hd in range(MEM_HEADS):
        sl = slice(hd * MEM_DIM, (hd + 1) * MEM_DIM)
        qn = _rms_rows(q[:, sl], gq_ref[...]) * (MEM_DIM ** -0.5)
        s = _dot_nt(qn, k_ref[0, hd])
        m = jnp.max(s, axis=-1, keepdims=True)
        p = jnp.exp(s - m)
        p = p / jnp.sum(p, axis=-1, keepdims=True)
        heads.append(_dot(p, v_ref[0, hd]).astype(BF16))
    o = jnp.concatenate(heads, axis=1)
    out_ref[0] = x1 + jnp.dot(o, wmo_ref[...], preferred_element_type=F32)


def _xattn(x, og, of, wo, gain, wq, gq, k, v, wmo):
    B, T, D = x.shape
    tm = TOK_TILE
    M = k.shape[2]
    row = lambda b, i: (b, i, 0)
    fixed = lambda b, i: (0, 0)
    kvs = lambda b, i: (b, 0, 0, 0)
    return pl.pallas_call(
        _xattn_kernel,
        grid=(B, T // tm),
        in_specs=[pl.BlockSpec((1, tm, D), row), pl.BlockSpec((1, tm, GDN_WIDTH), row),
                  pl.BlockSpec((1, tm, FOX_WIDTH), row), pl.BlockSpec((D, D), fixed),
                  pl.BlockSpec((1, D), fixed), pl.BlockSpec((D, D), fixed), pl.BlockSpec((1, MEM_DIM), fixed),
                  pl.BlockSpec((1, MEM_HEADS, M, MEM_DIM), kvs), pl.BlockSpec((1, MEM_HEADS, M, MEM_DIM), kvs),
                  pl.BlockSpec((D, D), fixed)],
        out_specs=pl.BlockSpec((1, tm, D), row),
        out_shape=jax.ShapeDtypeStruct((B, T, D), F32),
        compiler_params=_params(("parallel", "parallel")),
        name="xattn",
    )(x, og, of, wo, gain, wq, gq, k, v, wmo)


def _sort_network(n):
    out = []
    p = 1
    while p < n:
        k = p
        while k >= 1:
            for j in range(k % p, n - k, 2 * k):
                for i in range(min(k, n - j - k)):
                    if (i + j) // (2 * p) == (i + j + k) // (2 * p):
                        out.append((i + j, i + j + k))
            k //= 2
        p *= 2
    return out


def _top_values(s, n, rows_ref):
    g = 8
    lists = [s[k * g:(k + 1) * g, :] for k in range(s.shape[0] // g)]
    for i, j in _sort_network(len(lists)):
        hi = jnp.maximum(lists[i], lists[j])
        lists[j] = jnp.minimum(lists[i], lists[j])
        lists[i] = hi
    lists.append(jnp.full(lists[0].shape, NEG_INF, F32))
    for r in range(n):
        m = jnp.max(lists[0], axis=0, keepdims=True)
        rows_ref[r:r + 1, :] = m
        popped = lists[0] == m
        for k in range(min(n - 1 - r, len(lists) - 1)):
            lists[k] = jnp.where(popped, lists[k + 1], lists[k])


def _count_rows(s, rows, out_ref, below):
    g = 16
    for k in range(s.shape[0] // g):
        sg = s[k * g:(k + 1) * g, :]
        cnt = jnp.zeros(sg.shape, F32)
        for r, row in enumerate(rows):
            cnt = jnp.where(sg < row if below else sg >= row, float(r + 1), cnt)
        out_ref[k * g:(k + 1) * g, :] = cnt.astype(out_ref.dtype)


PEER_NSEL = PEER_TOPK + 1
PEER_PAIRS = [(i, j) for i in range(PEER_NSEL) for j in range(PEER_NSEL) if (i + 1) * (j + 1) <= PEER_NSEL]
PEER_CAND_ROWS = 64
assert len(PEER_PAIRS) <= PEER_CAND_ROWS
PEER_TOP_ROWS = -(-PEER_NSEL // 8) * 8


def _peer_kernel(x_ref, g_ref, wqt_ref, keys_ref, u_ref, vt_ref, out_ref,
                 h_s, n_s, e1_s, e2_s, rank_s, top_s, cand_s, acc_s, sc_s, w_s, qt_s):
    tm = PEER_TOK
    j = pl.program_id(1)
    nj = pl.num_programs(1)

    @pl.when(j == 0)
    def _():
        h = _rms_rows(x_ref[...], g_ref[...]).astype(BF16)
        h_s[...] = h
        acc_s[...] = jnp.zeros(acc_s.shape, F32)
        cand_s[...] = jnp.full(cand_s.shape, NEG_INF, F32)
        qt_s[...] = _dot_nt(wqt_ref[...], h).astype(BF16)
        for hd in range(PEER_HEADS):
            scores = []
            for half in range(2):
                r0 = (hd * 2 + half) * PEER_HALF
                scores.append(jnp.dot(keys_ref[half, hd], qt_s[r0:r0 + PEER_HALF, :],
                                      preferred_element_type=F32))
            s1, s2 = scores
            _top_values(s1, PEER_NSEL, top_s.at[0])
            _top_values(s2, PEER_NSEL, top_s.at[1])
            _count_rows(s2, [top_s[1, r:r + 1, :] for r in range(PEER_NSEL)], rank_s.at[hd], below=True)
            for r, (i1, i2) in enumerate(PEER_PAIRS):
                cand_s[r:r + 1, :] = top_s[0, i1:i1 + 1, :] + top_s[1, i2:i2 + 1, :]
            cand = cand_s[...]
            _top_values(cand, PEER_NSEL, top_s.at[2])
            tau = 0.5 * (top_s[2, PEER_TOPK - 1:PEER_TOPK, :] + top_s[2, PEER_TOPK:PEER_TOPK + 1, :])
            v1max = top_s[0, 0:1, :]
            v2max = top_s[1, 0:1, :]
            z = jnp.sum(jnp.where(cand >= tau, jnp.exp(cand - (v1max + v2max)), 0.0), axis=0, keepdims=True)
            e1_s[hd] = jnp.exp(s1 - v1max)
            e2_s[hd] = (jnp.exp(s2 - v2max) / z).astype(BF16)
            _count_rows(s1, [tau - top_s[1, r:r + 1, :] for r in range(PEER_TOPK)], n_s.at[hd], below=False)

    hb = h_s[...]
    n_sub = PEER_ECHUNK // PEER_SUB
    a_per_sub = PEER_SUB // PEER_KEYS
    rep = PEER_KEYS // 16

    def score(sub):
        sc_s[sub % 2] = _dot_nt(u_ref[sub * PEER_SUB:(sub + 1) * PEER_SUB, :], hb)

    score(0)
    for sub in range(n_sub):
        slot = sub % 2
        if sub + 1 < n_sub:
            score(sub + 1)
        for ai in range(a_per_sub):
            a_idx = j * (PEER_ECHUNK // PEER_KEYS) + sub * a_per_sub + ai
            ar = slice(ai * PEER_KEYS, (ai + 1) * PEER_KEYS)
            gate = None
            for hd in range(PEER_HEADS):
                n16 = jnp.broadcast_to(n_s[hd, pl.ds(a_idx, 1), :], (16, tm)).astype(BF16)
                e16 = jnp.broadcast_to(e1_s[hd, pl.ds(a_idx, 1), :], (16, tm)).astype(BF16)
                n_a = jnp.concatenate([n16] * rep, axis=0)
                e1_a = jnp.concatenate([e16] * rep, axis=0)
                term = jnp.where(rank_s[hd] < n_a, e1_a * e2_s[hd], jnp.zeros((), BF16))
                gate = term if gate is None else gate + term
            w_s[slot, ar, :] = _gelu_tanh(sc_s[slot, ar, :]).astype(BF16) * gate
        rows = slice(sub * PEER_SUB, (sub + 1) * PEER_SUB)
        acc_s[...] += jnp.dot(vt_ref[:, rows], w_s[slot], preferred_element_type=F32)

    @pl.when(j == nj - 1)
    def _():
        out_ref[...] = x_ref[...] + acc_s[...].T


def _peer(x2d, gain, wqt, keys, u, vt):
    M, D = x2d.shape
    tm = PEER_TOK
    E = u.shape[0]
    ec = PEER_ECHUNK
    return pl.pallas_call(
        _peer_kernel,
        grid=(M // tm, E // ec),
        in_specs=[pl.BlockSpec((tm, D), lambda i, j: (i, 0)), pl.BlockSpec((1, D), lambda i, j: (0, 0)),
                  pl.BlockSpec((PEER_HEADS * 2 * PEER_HALF, D), lambda i, j: (0, 0)),
                  pl.BlockSpec((2, PEER_HEADS, PEER_KEYS, PEER_HALF), lambda i, j: (0, 0, 0, 0)),
                  pl.BlockSpec((ec, D), lambda i, j: (j, 0)), pl.BlockSpec((D, ec), lambda i, j: (0, j))],
        out_specs=pl.BlockSpec((tm, D), lambda i, j: (i, 0)),
        out_shape=jax.ShapeDtypeStruct((M, D), F32),
        scratch_shapes=[pltpu.VMEM((tm, D), BF16),
                        pltpu.VMEM((PEER_HEADS, PEER_KEYS, tm), F32),
                        pltpu.VMEM((PEER_HEADS, PEER_KEYS, tm), F32),
                        pltpu.VMEM((PEER_HEADS, PEER_KEYS, tm), BF16),
                        pltpu.VMEM((PEER_HEADS, PEER_KEYS, tm), BF16),
                        pltpu.VMEM((3, PEER_TOP_ROWS, tm), F32),
                        pltpu.VMEM((PEER_CAND_ROWS, tm), F32),
                        pltpu.VMEM((D, tm), F32),
                        pltpu.VMEM((2, PEER_SUB, tm), F32),
                        pltpu.VMEM((2, PEER_SUB, tm), BF16),
                        pltpu.VMEM((PEER_HEADS * 2 * PEER_HALF, tm), BF16)],
        compiler_params=_params(("parallel", "arbitrary")),
        name="peer",
    )(x2d, gain, wqt, keys, u, vt)


def _lane_rep(v, width):
    return jnp.repeat(v.astype(F32), width)[None, :]


def _fox_layout(w_heads):
    D = w_heads.shape[0]
    out = jnp.zeros((D, FOX_HEADS, LANES), F32)
    out = out.at[:, :, :FOX_DIM].set(w_heads.reshape(D, FOX_HEADS, FOX_DIM))
    return out.reshape(D, FOX_HEADS * LANES).astype(BF16)


def _placement(lane_hi, lane_lo, sign):
    p = jnp.zeros((FOX_HEADS, 2 * LANES, LANES), F32)
    hs = jnp.arange(FOX_HEADS)
    p = p.at[hs, hs, lane_hi].set(sign)
    p = p.at[hs, LANES + hs, lane_lo].set(sign)
    return p.transpose(1, 0, 2).reshape(2 * LANES, FOX_HEADS * LANES).astype(BF16)


def _lane_const(lanes):
    c = jnp.zeros((1, LANES), F32)
    return c.at[0, jnp.array(lanes)].set(1.0)


def _pad_lanes(v, scale=1.0):
    return jnp.zeros((1, LANES), F32).at[0, :v.shape[0]].set(v.astype(F32) * scale)


def kernel(x, mem, mix_norm, w_in, conv_w, a_log, dt_bias, gdn_out_norm, fox_q_norm, fox_k_norm, fox_f_bias,
           w_out, xattn_norm, mem_norm, w_mq, w_mkv, mq_norm, mk_norm, w_mo, ffn_norm, w_pq, sub_keys,
           expert_u, expert_v):
    B, T, D = x.shape
    depth = w_in.shape[0]
    o1 = 3 * GDN_WIDTH
    o2 = o1 + GDN_WIDTH
    o3 = o2 + GDN_HEADS
    o4 = o3 + GDN_HEADS
    o5 = o4 + 3 * FOX_WIDTH
    nb = T // FOX_BLK
    place_q = _placement(FOX_DIM, FOX_DIM + 1, 1.0)
    place_k = _placement(FOX_DIM + 2, FOX_DIM + 3, -1.0)
    const_q = _lane_const([FOX_DIM + 2, FOX_DIM + 3])
    const_k = _lane_const([FOX_DIM, FOX_DIM + 1])

    for l in range(depth):
        wi = w_in[l]
        w_gdn = jnp.concatenate([wi[:, :o2], jnp.repeat(wi[:, o2:o3], GDN_DIM, axis=1),
                                 jnp.repeat(wi[:, o3:o4], GDN_DIM, axis=1)], axis=1).astype(BF16)
        aneg = _lane_rep(-jnp.exp(a_log[l].astype(F32)), GDN_DIM)
        dtb = _lane_rep(dt_bias[l], GDN_DIM)
        qkv, gate, la, beta = _gdn_in(x, mix_norm[l][None, :], w_gdn, aneg, dtb)
        o_gdn = _gdn(qkv, la, beta, gate, conv_w[l].reshape(CONV_K, 3 * GDN_WIDTH),
                     gdn_out_norm[l])
        fq_w = _fox_layout(wi[:, o4:o4 + FOX_WIDTH])
        fk_w = _fox_layout(wi[:, o4 + FOX_WIDTH:o4 + 2 * FOX_WIDTH])
        fv_w = wi[:, o4 + 2 * FOX_WIDTH:o5].astype(BF16)
        ff_w = jnp.zeros((D, LANES), F32).at[:, :FOX_HEADS].set(wi[:, o5:]).astype(BF16)
        q_aug, k_aug, v_fox, cb = _fox_in(
            x, mix_norm[l][None, :], fq_w, fk_w, fv_w, ff_w, _pad_lanes(fox_f_bias[l]),
            _pad_lanes(fox_q_norm[l], FOX_DIM ** -0.5 * LOG2E), _pad_lanes(fox_k_norm[l]),
            place_q, place_k, const_q, const_k)
        cb_heads = jnp.broadcast_to(
            jnp.transpose(cb[:, :, 0, :FOX_HEADS], (0, 2, 1))[..., None], (B, FOX_HEADS, nb, LANES))
        o_fox = _fox_attn(q_aug, k_aug, v_fox, cb_heads)
        k_mem, v_mem = _mem_kv(mem, mem_norm[l][None, :], w_mkv[l].astype(BF16), mk_norm[l][None, :])
        x = _xattn(x, o_gdn, o_fox, w_out[l].astype(BF16), xattn_norm[l][None, :], w_mq[l].astype(BF16),
                   mq_norm[l][None, :], k_mem, v_mem, w_mo[l].astype(BF16))
        x = _peer(x.reshape(B * T, D), ffn_norm[l][None, :], w_pq[l].T.astype(BF16),
                  sub_keys[l].astype(BF16), expert_u[l].astype(BF16),
                  expert_v[l].T.astype(BF16)).reshape(B, T, D)
    return x
```
